```python
import jax, jax.numpy as jnp
from jax import lax
import numpy as np

D_MODEL = 1024
BATCH = 8
SEQ = 4096
DEPTH = 1

N_META = 16
MIX_WIDTH = D_MODEL
ATTN_WIDTH = MIX_WIDTH // 2
CONV_WIDTH = MIX_WIDTH - ATTN_WIDTH
HEAD_DIM = 64
N_ATTN_HEADS = ATTN_WIDTH // HEAD_DIM
Q_BLOCK = 128
CONV_K = 3
N_EXPERTS = 32
TOP_K = 4
D_FF = D_MODEL
SWIGLU_LIMIT = 7.0
SWIGLU_ALPHA = 1.702
MOE_BLOCK = 128
RMS_EPS = 1e-6
PROJ_WIDTH = 3 * ATTN_WIDTH + 3 * CONV_WIDTH

kernel_name = "hymba_stickbreak_shortconv_moe"


def rms_norm(x, g):
    xf = x.astype(jnp.float32)
    y = xf * lax.rsqrt(jnp.mean(xf * xf, axis=-1, keepdims=True) + RMS_EPS)
    return (y * g.astype(jnp.float32)).astype(x.dtype)


def stick_breaking_attention(q, k, v):
    b, h, t, d = q.shape
    pad = (-N_META) % Q_BLOCK
    n_blocks = -(-(pad + t) // Q_BLOCK)
    tail = n_blocks * Q_BLOCK - pad - t
    padw = ((0, 0), (0, 0), (pad, tail), (0, 0))
    qp, kp, vp = [jnp.pad(a, padw) for a in (q, k, v)]
    scale = d ** -0.5
    outs = []
    for i in range(n_blocks):
        q0, q1 = i * Q_BLOCK, (i + 1) * Q_BLOCK
        qb = qp[:, :, q0:q1]
        kb = kp[:, :, :q1]
        vb = vp[:, :, :q1]
        z = jnp.einsum('bhqd,bhkd->bhqk', qb, kb).astype(jnp.float32) * scale
        qpos = jnp.arange(q0, q1)[:, None]
        kpos = jnp.arange(q1)[None, :]
        valid = (kpos < qpos) & (kpos >= pad)
        log_keep = jnp.where(valid, -jax.nn.softplus(z), 0.0)
        later = lax.cumsum(log_keep, axis=3, reverse=True) - log_keep
        log_w = jax.nn.log_sigmoid(z) + later
        w = jnp.where(valid, jnp.exp(log_w), 0.0)
        outs.append(jnp.einsum('bhqk,bhkd->bhqd', w.astype(vb.dtype), vb))
    out = jnp.concatenate(outs, axis=2)
    return out[:, :, pad:pad + t]


def short_conv_mixer(xin, gate_b, gate_c, conv_w, conv_b):
    u = gate_c * xin
    y = lax.conv_general_dilated(
        u, conv_w[:, None, :].astype(u.dtype), window_strides=(1,),
        padding=[(CONV_K - 1, 0)], dimension_numbers=('NWC', 'WIO', 'NWC'),
        feature_group_count=u.shape[-1])
    return gate_b * (y + conv_b)


def clamped_swiglu(gu):
    gate, lin = jnp.split(gu, 2, axis=-1)
    gate = jnp.minimum(gate, SWIGLU_LIMIT)
    lin = jnp.clip(lin, -SWIGLU_LIMIT, SWIGLU_LIMIT)
    return (lin + 1.0) * (gate * jax.nn.sigmoid(SWIGLU_ALPHA * gate))


def moe(x, w_router, b_router, w_gu, b_gu, w_down, b_down):
    d = x.shape[-1]
    xf = x.reshape(-1, d)
    n = xf.shape[0]
    logits = (xf @ w_router).astype(jnp.float32) + b_router.astype(jnp.float32)
    top_logit, top_e = lax.top_k(logits, TOP_K)
    gates = jax.nn.softmax(top_logit, axis=-1)
    a = n * TOP_K
    e_flat = top_e.reshape(-1)
    order = jnp.argsort(e_flat)
    e_sorted = e_flat[order]
    counts = jnp.zeros((N_EXPERTS,), jnp.int32).at[e_flat].add(1)
    padded = (counts + MOE_BLOCK - 1) // MOE_BLOCK * MOE_BLOCK
    start = jnp.cumsum(counts) - counts
    pend = jnp.cumsum(padded)
    pstart = pend - padded
    dest = pstart[e_sorted] + (jnp.arange(a, dtype=jnp.int32) - start[e_sorted])
    n_blocks = -(-a // MOE_BLOCK) + N_EXPERTS
    cap = n_blocks * MOE_BLOCK
    slot_src = jnp.full((cap,), a, jnp.int32).at[dest].set(order.astype(jnp.int32))
    slot_tok = slot_src // TOP_K
    slot_gate = jnp.concatenate([gates.reshape(-1), jnp.zeros((1,), jnp.float32)])[slot_src]
    x_pad = jnp.concatenate([xf, jnp.zeros((1, d), xf.dtype)], axis=0)
    xs = x_pad[slot_tok].reshape(n_blocks, MOE_BLOCK, d)
    block_start = jnp.arange(n_blocks, dtype=jnp.int32) * MOE_BLOCK
    block_expert = jnp.clip(jnp.searchsorted(pend, block_start, side='right'), 0, N_EXPERTS - 1)

    def expert_block(args):
        xb, e = args
        hid = clamped_swiglu(xb @ w_gu[e] + b_gu[e])
        return hid @ w_down[e] + b_down[e]

    ys = lax.map(expert_block, (xs, block_expert)).reshape(cap, d)
    ys = ys * slot_gate[:, None].astype(ys.dtype)
    y = jax.ops.segment_sum(ys, slot_tok, num_segments=n + 1)[:n]
    return y.reshape(x.shape)


def setup_inputs(seed: int = 0) -> dict:
    key = jax.random.key(seed)
    ks = jax.random.split(key, 20)
    f32 = jnp.float32
    nrm = lambda k, s, sc: jax.random.normal(k, s, f32) * sc
    return {
        "x": nrm(ks[0], (BATCH, SEQ, D_MODEL), 1.0),
        "meta_tokens": nrm(ks[1], (N_META, D_MODEL), 1.0),
        "g_mix": 1.0 + nrm(ks[2], (DEPTH, D_MODEL), 0.02),
        "w_in": nrm(ks[3], (DEPTH, D_MODEL, PROJ_WIDTH), D_MODEL ** -0.5),
        "g_attn_out": 1.0 + nrm(ks[4], (DEPTH, ATTN_WIDTH), 0.02),
        "conv_w": nrm(ks[5], (DEPTH, CONV_K, CONV_WIDTH), CONV_K ** -0.5),
        "conv_b": nrm(ks[6], (DEPTH, CONV_WIDTH), 0.02),
        "g_conv_out": 1.0 + nrm(ks[7], (DEPTH, CONV_WIDTH), 0.02),
        "w_out": nrm(ks[8], (DEPTH, MIX_WIDTH, D_MODEL), MIX_WIDTH ** -0.5),
        "g_ffn": 1.0 + nrm(ks[9], (DEPTH, D_MODEL), 0.02),
        "w_router": nrm(ks[10], (DEPTH, D_MODEL, N_EXPERTS), D_MODEL ** -0.5),
        "b_router": nrm(ks[11], (DEPTH, N_EXPERTS), 0.01),
        "w_gu": nrm(ks[12], (DEPTH, N_EXPERTS, D_MODEL, 2 * D_FF), D_MODEL ** -0.5),
        "b_gu": nrm(ks[13], (DEPTH, N_EXPERTS, 2 * D_FF), 0.01),
        "w_down": nrm(ks[14], (DEPTH, N_EXPERTS, D_FF, D_MODEL), D_FF ** -0.5),
        "b_down": nrm(ks[15], (DEPTH, N_EXPERTS, D_MODEL), 0.01),
        "g_final": 1.0 + nrm(ks[16], (D_MODEL,), 0.02),
    }


def reference(x, meta_tokens, g_mix, w_in, g_attn_out, conv_w, conv_b, g_conv_out,
              w_out, g_ffn, w_router, b_router, w_gu, b_gu, w_down, b_down, g_final):
    b = x.shape[0]
    meta = jnp.broadcast_to(meta_tokens[None].astype(x.dtype), (b, N_META, x.shape[-1]))
    h = jnp.concatenate([meta, x], axis=1)
    t = h.shape[1]
    A, C = ATTN_WIDTH, CONV_WIDTH
    cuts = [A, 2 * A, 3 * A, 3 * A + C, 3 * A + 2 * C]
    for l in range(DEPTH):
        n = rms_norm(h, g_mix[l])
        proj = n @ w_in[l]
        q, k, v, cb, cc, cx = jnp.split(proj, cuts, axis=-1)
        to_heads = lambda a: a.reshape(b, t, N_ATTN_HEADS, HEAD_DIM).transpose(0, 2, 1, 3)
        att = stick_breaking_attention(to_heads(q), to_heads(k), to_heads(v))
        att = att.transpose(0, 2, 1, 3).reshape(b, t, A)
        conv = short_conv_mixer(cx, cb, cc, conv_w[l], conv_b[l])
        mixed = jnp.concatenate([rms_norm(att, g_attn_out[l]),
                                 rms_norm(conv, g_conv_out[l])], axis=-1)
        h = h + mixed @ w_out[l]
        h = h + moe(rms_norm(h, g_ffn[l]), w_router[l], b_router[l],
                    w_gu[l], b_gu[l], w_down[l], b_down[l])
    return rms_norm(h, g_final)[:, N_META:]
```

```python
import functools

import jax
import jax.numpy as jnp
from jax import lax
from jax.experimental import pallas as pl
from jax.experimental.pallas import tpu as pltpu

N_META = 16
HEAD_DIM = 64
CONV_K = 3
TOP_K = 4
SWIGLU_LIMIT = 7.0
SWIGLU_ALPHA = 1.702
RMS_EPS = 1e-6

SUBLANES = 8
ROW_BLOCK = 256
KEY_CHUNK = 128
KEY_GROUPS = KEY_CHUNK // SUBLANES
MOE_ROWS = 256
CARRY_ROWS = SUBLANES
VMEM_LIMIT = 48 * 1024 * 1024

F32 = jnp.float32
BF16 = jnp.bfloat16


def _rms(x, g):
    return x * lax.rsqrt(jnp.mean(x * x, axis=-1, keepdims=True) + RMS_EPS) * g


def _in_proj_kernel(x_ref, meta_ref, g_ref, wqt_ref, wk_ref, wvt_ref, wc_ref, cw_ref, cb_ref, gc_ref,
                    qt_ref, k_ref, vt_ref, conv_ref, u_scr):
    j = pl.program_id(1)
    rows = x_ref.shape[1]
    cw = conv_ref.shape[2]
    h = jnp.where(j == 0, meta_ref[...], x_ref[0])
    nb = _rms(h, g_ref[...]).astype(BF16)
    nt_dims = (((1,), (1,)), ((), ()))
    qt_ref[0] = lax.dot_general(wqt_ref[...], nb, nt_dims, preferred_element_type=F32).astype(BF16)
    vt_ref[0] = lax.dot_general(wvt_ref[...], nb, nt_dims, preferred_element_type=F32).astype(BF16)
    k_ref[0] = jnp.dot(nb, wk_ref[...], preferred_element_type=F32).astype(BF16)
    cp = jnp.dot(nb, wc_ref[...], preferred_element_type=F32)
    gate_b, u = cp[:, :cw], cp[:, cw:2 * cw] * cp[:, 2 * cw:]

    @pl.when(j == 0)
    def _():
        u_scr[0:CARRY_ROWS, :] = jnp.zeros((CARRY_ROWS, cw), F32)

    u_scr[CARRY_ROWS:CARRY_ROWS + rows, :] = u
    u1 = u_scr[CARRY_ROWS - 1:CARRY_ROWS - 1 + rows, :]
    u2 = u_scr[CARRY_ROWS - 2:CARRY_ROWS - 2 + rows, :]
    y = cw_ref[0:1, :] * u2 + cw_ref[1:2, :] * u1 + cw_ref[2:3, :] * u
    u_scr[0:CARRY_ROWS, :] = u_scr[rows:rows + CARRY_ROWS, :]
    conv = gate_b * (y + cb_ref[...])
    conv_ref[0] = _rms(conv, gc_ref[...]).astype(BF16)


def _in_proj(x, meta_blk, g_mix, wqt, wk, wvt, wc, conv_w, conv_b, g_conv):
    b, seq, d = x.shape
    a = wk.shape[1]
    cw = wc.shape[1] // 3
    nblk = seq // ROW_BLOCK + 1
    tp = nblk * ROW_BLOCK
    full = lambda shp: pl.BlockSpec(shp, lambda bi, j: (0,) * len(shp))
    return pl.pallas_call(
        _in_proj_kernel,
        grid=(b, nblk),
        in_specs=[
            pl.BlockSpec((1, ROW_BLOCK, d), lambda bi, j: (bi, jnp.maximum(j - 1, 0), 0)),
            full((ROW_BLOCK, d)), full((1, d)), full((a, d)), full((d, a)), full((a, d)), full((d, 3 * cw)),
            full((CONV_K, cw)), full((1, cw)), full((1, cw)),
        ],
        out_specs=[
            pl.BlockSpec((1, a, ROW_BLOCK), lambda bi, j: (bi, 0, j)),
            pl.BlockSpec((1, ROW_BLOCK, a), lambda bi, j: (bi, j, 0)),
            pl.BlockSpec((1, a, ROW_BLOCK), lambda bi, j: (bi, 0, j)),
            pl.BlockSpec((1, ROW_BLOCK, cw), lambda bi, j: (bi, j, 0)),
        ],
        out_shape=[
            jax.ShapeDtypeStruct((b, a, tp), BF16),
            jax.ShapeDtypeStruct((b, tp, a), BF16),
            jax.ShapeDtypeStruct((b, a, tp), BF16),
            jax.ShapeDtypeStruct((b, tp, cw), BF16),
        ],
        scratch_shapes=[pltpu.VMEM((CARRY_ROWS + ROW_BLOCK, cw), F32)],
        compiler_params=pltpu.CompilerParams(
            dimension_semantics=("parallel", "arbitrary"), vmem_limit_bytes=VMEM_LIMIT),
        name="in_proj",
    )(x, meta_blk, g_mix, wqt, wk, wvt, wc, conv_w, conv_b, g_conv)


def _attn_kernel(qt_ref, k_ref, vt_ref, o_ref):
    i = pl.program_id(2)
    qt = qt_ref[0]
    hd, nq = qt.shape
    shape3 = (KEY_GROUPS, SUBLANES, nq)
    key_in_chunk = (lax.broadcasted_iota(jnp.int32, shape3, 1) * KEY_GROUPS
                    + lax.broadcasted_iota(jnp.int32, shape3, 0))
    qpos = i * nq + lax.broadcasted_iota(jnp.int32, shape3, 2)
    sub = lax.broadcasted_iota(jnp.int32, (SUBLANES, nq), 0)

    def chunk(j, carry, acc, masked):
        z = jnp.dot(k_ref[0, 0, j], qt, preferred_element_type=F32).reshape(shape3)
        sp = jnp.maximum(z, 0.0) + jnp.log(1.0 + jnp.exp(-jnp.abs(z)))
        if masked:
            valid = j * KEY_CHUNK + key_in_chunk < qpos
            sp = jnp.where(valid, sp, 0.0)
        runs = [None] * KEY_GROUPS
        run = sp[KEY_GROUPS - 1]
        runs[KEY_GROUPS - 1] = run
        for g in range(KEY_GROUPS - 2, -1, -1):
            run = run + sp[g]
            runs[g] = run
        seg = run
        off = jnp.broadcast_to(carry, (SUBLANES, nq))
        for r in range(1, SUBLANES):
            off = off + jnp.where(sub < r, seg[r:r + 1, :], 0.0)
        log_w = z - (jnp.stack(runs, axis=0) + off[None])
        w = jnp.exp(log_w)
        if masked:
            w = jnp.where(valid, w, 0.0)
        wt = w.reshape(KEY_CHUNK, nq).astype(BF16)
        acc = acc + jnp.dot(vt_ref[0, 0, j], wt, preferred_element_type=F32)
        return off[0:1, :] + seg[0:1, :], acc

    carry = jnp.zeros((1, nq), F32)
    acc = jnp.zeros((hd, nq), F32)
    chunks_per_q = nq // KEY_CHUNK
    top = (i + 1) * chunks_per_q - 1
    for c in range(chunks_per_q):
        carry, acc = chunk(top - c, carry, acc, True)

    def body(t, state):
        return chunk(i * chunks_per_q - 1 - t, state[0], state[1], False)

    carry, acc = lax.fori_loop(0, jnp.maximum(i * chunks_per_q - 1, 0), body, (carry, acc))
    o_ref[0] = acc


def _attention(qt, k5, v5):
    b, a, tp = qt.shape
    heads = a // HEAD_DIM
    nchunks = tp // KEY_CHUNK
    return pl.pallas_call(
        _attn_kernel,
        grid=(b, heads, tp // ROW_BLOCK),
        in_specs=[
            pl.BlockSpec((1, HEAD_DIM, ROW_BLOCK), lambda bi, h, i: (bi, h, i)),
            pl.BlockSpec((1, 1, nchunks, KEY_CHUNK, HEAD_DIM), lambda bi, h, i: (bi, h, 0, 0, 0)),
            pl.BlockSpec((1, 1, nchunks, HEAD_DIM, KEY_CHUNK), lambda bi, h, i: (bi, h, 0, 0, 0)),
        ],
        out_specs=pl.BlockSpec((1, HEAD_DIM, ROW_BLOCK), lambda bi, h, i: (bi, h, i)),
        out_shape=jax.ShapeDtypeStruct((b, a, tp), F32),
        compiler_params=pltpu.CompilerParams(
            dimension_semantics=("parallel", "parallel", "parallel"), vmem_limit_bytes=VMEM_LIMIT),
        name="stickbreak_attn",
    )(qt, k5, v5)


def _top_k_gates(logits):
    n, e = logits.shape
    lane = lax.broadcasted_iota(jnp.int32, (n, e), 1)
    slot = lax.broadcasted_iota(jnp.int32, (n, TOP_K), 1)
    top_e = jnp.zeros((n, TOP_K), jnp.int32)
    top_l = jnp.zeros((n, TOP_K), F32)
    cur = logits
    for kk in range(TOP_K):
        m = jnp.max(cur, axis=-1, keepdims=True)
        idx = jnp.min(jnp.where(cur == m, lane, e), axis=-1, keepdims=True)
        top_e = jnp.where(slot == kk, idx, top_e)
        top_l = jnp.where(slot == kk, m, top_l)
        cur = jnp.where(lane == idx, -jnp.inf, cur)
    ex = jnp.exp(top_l - top_l[:, 0:1])
    return top_e, ex / jnp.sum(ex, axis=-1, keepdims=True)


def _post_kernel(at_ref, conv_ref, h_ref, ga_ref, wa_ref, wcv_ref, gf_ref, wr_ref, br_ref, *rest, meta):
    h1_ref, xn_ref, te_ref, gt_ref = rest[-4:]
    att = at_ref[0].T
    convn = conv_ref[0]
    h = h_ref[...] if meta else h_ref[0]
    if meta:
        att, convn = att[ROW_BLOCK - N_META:, :], convn[ROW_BLOCK - N_META:, :]
    attn = _rms(att, ga_ref[...]).astype(BF16)
    h1 = (h + jnp.dot(attn, wa_ref[...], preferred_element_type=F32)
          + jnp.dot(convn, wcv_ref[...], preferred_element_type=F32))
    xn = _rms(h1, gf_ref[...])
    logits = jnp.dot(xn, wr_ref[...], preferred_element_type=F32, precision=lax.Precision.HIGHEST) + br_ref[...]
    top_e, gates = _top_k_gates(logits)
    h1_ref[...] = h1
    xn_ref[...] = xn
    te_ref[...] = top_e
    gt_ref[...] = gates


def _post(att_t, convn, x, meta_tokens, g_attn, w_out_a, w_out_c, g_ffn, w_router, b_router):
    b, seq, d = x.shape
    a = att_t.shape[1]
    cw = convn.shape[2]
    e = w_router.shape[1]
    nreal = b * seq
    ntok = nreal + b * N_META
    nblk = seq // ROW_BLOCK
    out_shape = [
        jax.ShapeDtypeStruct((ntok, d), F32),
        jax.ShapeDtypeStruct((ntok, d), F32),
        jax.ShapeDtypeStruct((ntok, TOP_K), jnp.int32),
        jax.ShapeDtypeStruct((ntok, TOP_K), F32),
    ]
    weights = (g_attn, w_out_a, w_out_c, g_ffn, w_router, b_router)

    def wspecs(nargs):
        return [pl.BlockSpec(w.shape, lambda *idx, n=w.ndim: (0,) * n) for w in weights]

    def ospecs(rows, index):
        return [pl.BlockSpec((rows, d), index), pl.BlockSpec((rows, d), index),
                pl.BlockSpec((rows, TOP_K), index), pl.BlockSpec((rows, TOP_K), index)]

    real = pl.pallas_call(
        functools.partial(_post_kernel, meta=False),
        grid=(b, nblk),
        in_specs=[
            pl.BlockSpec((1, a, ROW_BLOCK), lambda bi, j: (bi, 0, j + 1)),
            pl.BlockSpec((1, ROW_BLOCK, cw), lambda bi, j: (bi, j + 1, 0)),
            pl.BlockSpec((1, ROW_BLOCK, d), lambda bi, j: (bi, j, 0)),
        ] + wspecs(2),
        out_specs=ospecs(ROW_BLOCK, lambda bi, j: (bi * nblk + j, 0)),
        out_shape=out_shape,
        compiler_params=pltpu.CompilerParams(
            dimension_semantics=("parallel", "parallel"), vmem_limit_bytes=VMEM_LIMIT),
        name="post_attn_real",
    )(att_t, convn, x, *weights)

    meta_blk0 = nreal // N_META
    return pl.pallas_call(
        functools.partial(_post_kernel, meta=True),
        grid=(b,),
        in_specs=[
            pl.BlockSpec((1, a, ROW_BLOCK), lambda bi: (bi, 0, 0)),
            pl.BlockSpec((1, ROW_BLOCK, cw), lambda bi: (bi, 0, 0)),
            pl.BlockSpec((N_META, d), lambda bi: (0, 0)),
        ] + wspecs(1) + [pl.BlockSpec(memory_space=pl.ANY)] * 4,
        out_specs=ospecs(N_META, lambda bi: (meta_blk0 + bi, 0)),
        out_shape=out_shape,
        input_output_aliases={9: 0, 10: 1, 11: 2, 12: 3},
        compiler_params=pltpu.CompilerParams(
            dimension_semantics=("parallel",), vmem_limit_bytes=VMEM_LIMIT),
        name="post_attn_meta",
    )(att_t, convn, meta_tokens, *weights, *real)


def _moe_kernel(bexp_ref, nused_ref, tok_ref, dst_ref, xn_hbm, wgu_ref, bgu_ref, wd_ref, bd_ref,
                ys_hbm, xbuf, ybuf, wgu_bf, wd_bf, sems):
    i = pl.program_id(0)
    rows = xbuf.shape[0]
    ff = wd_bf.shape[0]

    def gather(r):
        return pltpu.make_async_copy(xn_hbm.at[pl.ds(tok_ref[0, 0, r], 1), :], xbuf.at[pl.ds(r, 1), :], sems.at[0])

    def scatter(r):
        return pltpu.make_async_copy(ybuf.at[pl.ds(r, 1), :], ys_hbm.at[pl.ds(dst_ref[0, 0, r], 1), :], sems.at[1])

    @pl.when(i < nused_ref[0])
    def _():
        @pl.loop(0, rows)
        def _(r):
            gather(r).start()

        @pl.when((i == 0) | (bexp_ref[i] != bexp_ref[jnp.maximum(i - 1, 0)]))
        def _():
            wgu_bf[...] = wgu_ref[0].astype(BF16)
            wd_bf[...] = wd_ref[0].astype(BF16)

        @pl.loop(0, rows)
        def _(r):
            gather(r).wait()

        gu = jnp.dot(xbuf[...].astype(BF16), wgu_bf[...], preferred_element_type=F32) + bgu_ref[0]
        gate = jnp.minimum(gu[:, :ff], SWIGLU_LIMIT)
        lin = jnp.clip(gu[:, ff:], -SWIGLU_LIMIT, SWIGLU_LIMIT)
        hid = (lin + 1.0) * (gate * jax.nn.sigmoid(SWIGLU_ALPHA * gate))
        ybuf[...] = jnp.dot(hid.astype(BF16), wd_bf[...], preferred_element_type=F32) + bd_ref[0]

        @pl.loop(0, rows)
        def _(r):
            scatter(r).start()

        @pl.loop(0, rows)
        def _(r):
            scatter(r).wait()


def _moe(xn, block_expert, n_used, slot_tok, slot_dst, w_gu, b_gu, w_down, b_down):
    ntok, d = xn.shape
    e, _, ff2 = w_gu.shape
    ff = ff2 // 2
    nblocks = block_expert.shape[0]
    grid_spec = pltpu.PrefetchScalarGridSpec(
        num_scalar_prefetch=2,
        grid=(nblocks,),
        in_specs=[
            pl.BlockSpec((1, 1, MOE_ROWS), lambda i, be, nu: (i, 0, 0), memory_space=pltpu.SMEM),
            pl.BlockSpec((1, 1, MOE_ROWS), lambda i, be, nu: (i, 0, 0), memory_space=pltpu.SMEM),
            pl.BlockSpec(memory_space=pl.ANY),
            pl.BlockSpec((1, d, ff2), lambda i, be, nu: (be[i], 0, 0)),
            pl.BlockSpec((1, 1, ff2), lambda i, be, nu: (be[i], 0, 0)),
            pl.BlockSpec((1, ff, d), lambda i, be, nu: (be[i], 0, 0)),
            pl.BlockSpec((1, 1, d), lambda i, be, nu: (be[i], 0, 0)),
        ],
        out_specs=pl.BlockSpec(memory_space=pl.ANY),
        scratch_shapes=[
            pltpu.VMEM((MOE_ROWS, d), F32),
            pltpu.VMEM((MOE_ROWS, d), F32),
            pltpu.VMEM((d, ff2), BF16),
            pltpu.VMEM((ff, d), BF16),
            pltpu.SemaphoreType.DMA((2,)),
        ],
    )
    return pl.pallas_call(
        _moe_kernel,
        grid_spec=grid_spec,
        out_shape=jax.ShapeDtypeStruct(((TOP_K + 1) * ntok, d), F32),
        compiler_params=pltpu.CompilerParams(
            dimension_semantics=("arbitrary",), vmem_limit_bytes=VMEM_LIMIT),
        name="moe_experts",
    )(block_expert, n_used, slot_tok.reshape(nblocks, 1, MOE_ROWS), slot_dst.reshape(nblocks, 1, MOE_ROWS),
      xn, w_gu, b_gu.reshape(e, 1, ff2), w_down, b_down.reshape(e, 1, d))


def _route(top_e, n_experts):
    ntok = top_e.shape[0]
    npairs = ntok * TOP_K
    e_flat = top_e.reshape(-1)
    order = jnp.argsort(e_flat).astype(jnp.int32)
    e_sorted = e_flat[order]
    counts = jnp.zeros((n_experts,), jnp.int32).at[e_flat].add(1)
    padded = (counts + MOE_ROWS - 1) // MOE_ROWS * MOE_ROWS
    start = jnp.cumsum(counts) - counts
    pend = jnp.cumsum(padded)
    pstart = pend - padded
    dest = pstart[e_sorted] + (jnp.arange(npairs, dtype=jnp.int32) - start[e_sorted])
    nblocks = -(-npairs // MOE_ROWS) + n_experts
    cap = nblocks * MOE_ROWS
    slot_src = jnp.full((cap,), -1, jnp.int32).at[dest].set(order)
    empty = slot_src < 0
    pair = jnp.maximum(slot_src, 0)
    slot_tok = jnp.where(empty, 0, pair // TOP_K)
    spare = npairs + jnp.arange(cap, dtype=jnp.int32) % MOE_ROWS
    slot_dst = jnp.where(empty, spare, (pair % TOP_K) * ntok + pair // TOP_K)
    block_start = jnp.arange(nblocks, dtype=jnp.int32) * MOE_ROWS
    block_expert = jnp.clip(jnp.searchsorted(pend, block_start, side="right"), 0, n_experts - 1).astype(jnp.int32)
    n_used = (pend[-1] // MOE_ROWS).astype(jnp.int32).reshape(1)
    return block_expert, n_used, slot_tok.astype(jnp.int32), slot_dst.astype(jnp.int32)


def _combine_kernel(h1_ref, gt_ref, y0_ref, y1_ref, y2_ref, y3_ref, g_ref, o_ref):
    gates = gt_ref[...]
    acc = h1_ref[...]
    for kk, y_ref in enumerate((y0_ref, y1_ref, y2_ref, y3_ref)):
        acc = acc + gates[:, kk:kk + 1] * y_ref[0]
    o_ref[...] = _rms(acc, g_ref[...])


def _combine(h1, gates, ys, g_final, nreal):
    ntok, d = h1.shape
    ys4 = ys.reshape(TOP_K + 1, ntok, d)
    yspec = lambda kk: pl.BlockSpec((1, ROW_BLOCK, d), lambda j, kk=kk: (kk, j, 0))
    return pl.pallas_call(
        _combine_kernel,
        grid=(nreal // ROW_BLOCK,),
        in_specs=[pl.BlockSpec((ROW_BLOCK, d), lambda j: (j, 0)),
                  pl.BlockSpec((ROW_BLOCK, TOP_K), lambda j: (j, 0)),
                  yspec(0), yspec(1), yspec(2), yspec(3),
                  pl.BlockSpec((1, d), lambda j: (0, 0))],
        out_specs=pl.BlockSpec((ROW_BLOCK, d), lambda j: (j, 0)),
        out_shape=jax.ShapeDtypeStruct((nreal, d), F32),
        compiler_params=pltpu.CompilerParams(
            dimension_semantics=("parallel",), vmem_limit_bytes=VMEM_LIMIT),
        name="moe_combine",
    )(h1, gates, ys4, ys4, ys4, ys4, g_final)


def kernel(x, meta_tokens, g_mix, w_in, g_attn_out, conv_w, conv_b, g_conv_out, w_out, g_ffn,
           w_router, b_router, w_gu, b_gu, w_down, b_down, g_final):
    b, seq, d = x.shape
    depth = w_in.shape[0]
    assert depth == 1 and seq % ROW_BLOCK == 0
    a = w_out.shape[1] // 2
    cw = w_out.shape[1] - a
    heads = a // HEAD_DIM
    n_experts = w_router.shape[2]
    tp = seq + ROW_BLOCK
    nchunks = tp // KEY_CHUNK

    meta_blk = jnp.concatenate([jnp.zeros((ROW_BLOCK - N_META, d), x.dtype), meta_tokens], axis=0)
    w = w_in[0]
    scale = HEAD_DIM ** -0.5
    wqt = (w[:, :a] * scale).T.astype(BF16)
    wk = w[:, a:2 * a].astype(BF16)
    wvt = w[:, 2 * a:3 * a].T.astype(BF16)
    wc = w[:, 3 * a:].astype(BF16)
    qt, k, vt, convn = _in_proj(x, meta_blk, g_mix, wqt, wk, wvt, wc, conv_w[0], conv_b, g_conv_out)

    k5 = (k.reshape(b, nchunks, SUBLANES, KEY_GROUPS, heads, HEAD_DIM)
          .transpose(0, 4, 1, 3, 2, 5).reshape(b, heads, nchunks, KEY_CHUNK, HEAD_DIM))
    v5 = (vt.reshape(b, heads, HEAD_DIM, nchunks, SUBLANES, KEY_GROUPS)
          .transpose(0, 1, 3, 2, 5, 4).reshape(b, heads, nchunks, HEAD_DIM, KEY_CHUNK))
    att_t = _attention(qt, k5, v5)

    wo = w_out[0].astype(BF16)
    h1, xn, top_e, gates = _post(att_t, convn, x, meta_tokens, g_attn_out, wo[:a], wo[a:], g_ffn,
                                 w_router[0], b_router)

    ntok = h1.shape[0]
    block_expert, n_used, slot_tok, slot_dst = _route(top_e, n_experts)
    ys = _moe(xn, block_expert, n_used, slot_tok, slot_dst, w_gu[0], b_gu[0], w_down[0], b_down[0])
    out = _combine(h1, gates, ys, g_final.reshape(1, d), b * seq)
    return out.reshape(b, seq, d)
```

```python
import functools

import jax
import jax.numpy as jnp
from jax import lax
from jax.experimental import pallas as pl
from jax.experimental.pallas import tpu as pltpu

N_META = 16
HEAD_DIM = 64
CONV_K = 3
TOP_K = 4
SWIGLU_LIMIT = 7.0
SWIGLU_ALPHA = 1.702
RMS_EPS = 1e-6

SUBLANES = 8
ROW_BLOCK = 256
KEY_CHUNK = 128
KEY_GROUPS = KEY_CHUNK // SUBLANES
ATTN_CHUNKS_PER_ITER = ROW_BLOCK // KEY_CHUNK
LOG2_E = 1.4426950408889634
MOE_ROWS = 256
CARRY_ROWS = SUBLANES
VMEM_LIMIT = 48 * 1024 * 1024

F32 = jnp.float32
BF16 = jnp.bfloat16


def _rms(x, g):
    return x * lax.rsqrt(jnp.mean(x * x, axis=-1, keepdims=True) + RMS_EPS) * g


def _in_proj_kernel(x_ref, meta_ref, g_ref, wqt_ref, wk_ref, wvt_ref, wc_ref, cw_ref, cb_ref, gc_ref,
                    qt_ref, k_ref, vt_ref, conv_ref, u_scr):
    j = pl.program_id(1)
    rows = x_ref.shape[1]
    cw = conv_ref.shape[2]
    h = jnp.where(j == 0, meta_ref[...], x_ref[0])
    nb = _rms(h, g_ref[...]).astype(BF16)
    nt_dims = (((1,), (1,)), ((), ()))
    qt_ref[0, 0] = lax.dot_general(wqt_ref[...], nb, nt_dims, preferred_element_type=F32).astype(BF16)
    vt_ref[0] = lax.dot_general(wvt_ref[...], nb, nt_dims, preferred_element_type=F32).astype(BF16)
    k_ref[0] = jnp.dot(nb, wk_ref[...], preferred_element_type=F32).astype(BF16)
    cp = jnp.dot(nb, wc_ref[...], preferred_element_type=F32)
    gate_b, u = cp[:, :cw], cp[:, cw:2 * cw] * cp[:, 2 * cw:]

    @pl.when(j == 0)
    def _():
        u_scr[0:CARRY_ROWS, :] = jnp.zeros((CARRY_ROWS, cw), F32)

    u_scr[CARRY_ROWS:CARRY_ROWS + rows, :] = u
    u1 = u_scr[CARRY_ROWS - 1:CARRY_ROWS - 1 + rows, :]
    u2 = u_scr[CARRY_ROWS - 2:CARRY_ROWS - 2 + rows, :]
    y = cw_ref[0:1, :] * u2 + cw_ref[1:2, :] * u1 + cw_ref[2:3, :] * u
    u_scr[0:CARRY_ROWS, :] = u_scr[rows:rows + CARRY_ROWS, :]
    conv = gate_b * (y + cb_ref[...])
    conv_ref[0] = _rms(conv, gc_ref[...]).astype(BF16)


def _in_proj(x, meta_blk, g_mix, wqt, wk, wvt, wc, conv_w, conv_b, g_conv):
    b, seq, d = x.shape
    a = wk.shape[1]
    cw = wc.shape[1] // 3
    nblk = seq // ROW_BLOCK + 1
    tp = nblk * ROW_BLOCK
    full = lambda shp: pl.BlockSpec(shp, lambda bi, j: (0,) * len(shp))
    return pl.pallas_call(
        _in_proj_kernel,
        grid=(b, nblk),
        in_specs=[
            pl.BlockSpec((1, ROW_BLOCK, d), lambda bi, j: (bi, jnp.maximum(j - 1, 0), 0)),
            full((ROW_BLOCK, d)), full((1, d)), full((a, d)), full((d, a)), full((a, d)), full((d, 3 * cw)),
            full((CONV_K, cw)), full((1, cw)), full((1, cw)),
        ],
        out_specs=[
            pl.BlockSpec((1, 1, a, ROW_BLOCK), lambda bi, j: (bi, j, 0, 0)),
            pl.BlockSpec((1, ROW_BLOCK, a), lambda bi, j: (bi, j, 0)),
            pl.BlockSpec((1, a, ROW_BLOCK), lambda bi, j: (bi, 0, j)),
            pl.BlockSpec((1, ROW_BLOCK, cw), lambda bi, j: (bi, j, 0)),
        ],
        out_shape=[
            jax.ShapeDtypeStruct((b, nblk, a, ROW_BLOCK), BF16),
            jax.ShapeDtypeStruct((b, tp, a), BF16),
            jax.ShapeDtypeStruct((b, a, tp), BF16),
            jax.ShapeDtypeStruct((b, tp, cw), BF16),
        ],
        scratch_shapes=[pltpu.VMEM((CARRY_ROWS + ROW_BLOCK, cw), F32)],
        compiler_params=pltpu.CompilerParams(
            dimension_semantics=("parallel", "arbitrary"), vmem_limit_bytes=VMEM_LIMIT),
        name="in_proj",
    )(x, meta_blk, g_mix, wqt, wk, wvt, wc, conv_w, conv_b, g_conv)


def _attn_kernel(qt_ref, k_ref, vt_ref, o_ref, z_scr, w_scr):
    nblk, hd, nq = qt_ref.shape[1:]
    per_iter = ATTN_CHUNKS_PER_ITER
    shape3 = (KEY_GROUPS, SUBLANES, nq)
    key_in_chunk = (lax.broadcasted_iota(jnp.int32, shape3, 1) * KEY_GROUPS
                    + lax.broadcasted_iota(jnp.int32, shape3, 0))
    lane = lax.broadcasted_iota(jnp.int32, shape3, 2)
    sub = lax.broadcasted_iota(jnp.int32, (SUBLANES, nq), 0)
    step_top = lambda i, t: (i + 1 - t) * per_iter - 1

    def scores(i, jtop):
        qt = qt_ref[0, i]
        for c in range(per_iter):
            z_scr[c] = jnp.dot(k_ref[0, 0, jtop - c], qt, preferred_element_type=F32)

    def weights(carry, i, jtop, masked):
        for c in range(per_iter):
            z = z_scr[c].reshape(shape3)
            sp = jnp.maximum(z, jnp.log(1.0 + jnp.exp2(jnp.minimum(z, 64.0))) * LOG2_E)
            if masked:
                valid = (jtop - c) * KEY_CHUNK + key_in_chunk < i * nq + lane
                sp = jnp.where(valid, sp, 0.0)
            runs = [None] * KEY_GROUPS
            run = sp[KEY_GROUPS - 1]
            runs[KEY_GROUPS - 1] = run
            for g in range(KEY_GROUPS - 2, -1, -1):
                run = run + sp[g]
                runs[g] = run
            seg = run
            off = jnp.broadcast_to(carry, (SUBLANES, nq))
            for r in range(1, SUBLANES):
                off = off + jnp.where(sub < r, seg[r:r + 1, :], 0.0)
            w = jnp.exp2(z - (jnp.stack(runs, axis=0) + off[None]))
            if masked:
                w = jnp.where(valid, w, 0.0)
            w_scr[c] = w.reshape(KEY_CHUNK, nq).astype(BF16)
            carry = off[0:1, :] + seg[0:1, :]
        return carry

    def weighted_values(acc, jtop):
        for c in range(per_iter):
            acc = acc + jnp.dot(vt_ref[0, 0, jtop - c], w_scr[c], preferred_element_type=F32)
        return acc

    w_scr[...] = jnp.zeros(w_scr.shape, BF16)
    scores(0, step_top(0, 0))

    def query_block(i, acc):
        acc = weighted_values(acc, per_iter - 1)
        o_ref[0, jnp.maximum(i - 1, 0)] = acc
        carry = weights(jnp.zeros((1, nq), F32), i, step_top(i, 0), True)
        nxt = jnp.minimum(i + 1, nblk - 1)
        scores(jnp.where(i > 0, i, nxt), jnp.where(i > 0, step_top(i, 1), step_top(nxt, 0)))

        def step(t, state):
            carry, acc = state
            acc = weighted_values(acc, step_top(i, t - 1))
            carry = weights(carry, i, step_top(i, t), False)
            scores(jnp.where(t < i, i, nxt), jnp.where(t < i, step_top(i, t + 1), step_top(nxt, 0)))
            return carry, acc

        return lax.fori_loop(1, i + 1, step, (carry, jnp.zeros((hd, nq), F32)))[1]

    acc = lax.fori_loop(0, nblk, query_block, jnp.zeros((hd, nq), F32))
    o_ref[0, nblk - 1] = weighted_values(acc, per_iter - 1)


def _attention(qt, k5, v5):
    b, nblk, a, nq = qt.shape
    heads = a // HEAD_DIM
    nchunks = k5.shape[2]
    return pl.pallas_call(
        _attn_kernel,
        grid=(b, heads),
        in_specs=[
            pl.BlockSpec((1, nblk, HEAD_DIM, nq), lambda bi, h: (bi, 0, h, 0)),
            pl.BlockSpec((1, 1, nchunks, KEY_CHUNK, HEAD_DIM), lambda bi, h: (bi, h, 0, 0, 0)),
            pl.BlockSpec((1, 1, nchunks, HEAD_DIM, KEY_CHUNK), lambda bi, h: (bi, h, 0, 0, 0)),
        ],
        out_specs=pl.BlockSpec((1, nblk, HEAD_DIM, nq), lambda bi, h: (bi, 0, h, 0)),
        out_shape=jax.ShapeDtypeStruct((b, nblk, a, nq), F32),
        scratch_shapes=[pltpu.VMEM((ATTN_CHUNKS_PER_ITER, KEY_CHUNK, nq), F32),
                        pltpu.VMEM((ATTN_CHUNKS_PER_ITER, KEY_CHUNK, nq), BF16)],
        compiler_params=pltpu.CompilerParams(
            dimension_semantics=("parallel", "parallel"), vmem_limit_bytes=VMEM_LIMIT),
        name="stickbreak_attn",
    )(qt, k5, v5)


def _top_k_gates(logits):
    n, e = logits.shape
    lane = lax.broadcasted_iota(jnp.int32, (n, e), 1)
    slot = lax.broadcasted_iota(jnp.int32, (n, TOP_K), 1)
    top_e = jnp.zeros((n, TOP_K), jnp.int32)
    top_l = jnp.zeros((n, TOP_K), F32)
    cur = logits
    for kk in range(TOP_K):
        m = jnp.max(cur, axis=-1, keepdims=True)
        idx = jnp.min(jnp.where(cur == m, lane, e), axis=-1, keepdims=True)
        top_e = jnp.where(slot == kk, idx, top_e)
        top_l = jnp.where(slot == kk, m, top_l)
        cur = jnp.where(lane == idx, -jnp.inf, cur)
    ex = jnp.exp(top_l - top_l[:, 0:1])
    return top_e, ex / jnp.sum(ex, axis=-1, keepdims=True)


def _post_kernel(at_ref, conv_ref, h_ref, ga_ref, wa_ref, wcv_ref, gf_ref, wr_ref, br_ref, *rest, meta):
    h1_ref, xn_ref, te_ref, gt_ref = rest[-4:]
    att = at_ref[0, 0].T
    convn = conv_ref[0]
    h = h_ref[...] if meta else h_ref[0]
    if meta:
        att, convn = att[ROW_BLOCK - N_META:, :], convn[ROW_BLOCK - N_META:, :]
    attn = _rms(att, ga_ref[...]).astype(BF16)
    h1 = (h + jnp.dot(attn, wa_ref[...], preferred_element_type=F32)
          + jnp.dot(convn, wcv_ref[...], preferred_element_type=F32))
    xn = _rms(h1, gf_ref[...])
    logits = jnp.dot(xn, wr_ref[...], preferred_element_type=F32, precision=lax.Precision.HIGHEST) + br_ref[...]
    top_e, gates = _top_k_gates(logits)
    h1_ref[...] = h1
    xn_ref[...] = xn
    te_ref[...] = top_e
    gt_ref[...] = gates


def _post(att_t, convn, x, meta_tokens, g_attn, w_out_a, w_out_c, g_ffn, w_router, b_router):
    b, seq, d = x.shape
    a = att_t.shape[2]
    cw = convn.shape[2]
    e = w_router.shape[1]
    nreal = b * seq
    ntok = nreal + b * N_META
    nblk = seq // ROW_BLOCK
    out_shape = [
        jax.ShapeDtypeStruct((ntok, d), F32),
        jax.ShapeDtypeStruct((ntok, d), F32),
        jax.ShapeDtypeStruct((ntok, TOP_K), jnp.int32),
        jax.ShapeDtypeStruct((ntok, TOP_K), F32),
    ]
    weights = (g_attn, w_out_a, w_out_c, g_ffn, w_router, b_router)

    def wspecs(nargs):
        return [pl.BlockSpec(w.shape, lambda *idx, n=w.ndim: (0,) * n) for w in weights]

    def ospecs(rows, index):
        return [pl.BlockSpec((rows, d), index), pl.BlockSpec((rows, d), index),
                pl.BlockSpec((rows, TOP_K), index), pl.BlockSpec((rows, TOP_K), index)]

    real = pl.pallas_call(
        functools.partial(_post_kernel, meta=False),
        grid=(b, nblk),
        in_specs=[
            pl.BlockSpec((1, 1, a, ROW_BLOCK), lambda bi, j: (bi, j + 1, 0, 0)),
            pl.BlockSpec((1, ROW_BLOCK, cw), lambda bi, j: (bi, j + 1, 0)),
            pl.BlockSpec((1, ROW_BLOCK, d), lambda bi, j: (bi, j, 0)),
        ] + wspecs(2),
        out_specs=ospecs(ROW_BLOCK, lambda bi, j: (bi * nblk + j, 0)),
        out_shape=out_shape,
        compiler_params=pltpu.CompilerParams(
            dimension_semantics=("parallel", "parallel"), vmem_limit_bytes=VMEM_LIMIT),
        name="post_attn_real",
    )(att_t, convn, x, *weights)

    meta_blk0 = nreal // N_META
    return pl.pallas_call(
        functools.partial(_post_kernel, meta=True),
        grid=(b,),
        in_specs=[
            pl.BlockSpec((1, 1, a, ROW_BLOCK), lambda bi: (bi, 0, 0, 0)),
            pl.BlockSpec((1, ROW_BLOCK, cw), lambda bi: (bi, 0, 0)),
            pl.BlockSpec((N_META, d), lambda bi: (0, 0)),
        ] + wspecs(1) + [pl.BlockSpec(memory_space=pl.ANY)] * 4,
        out_specs=ospecs(N_META, lambda bi: (meta_blk0 + bi, 0)),
        out_shape=out_shape,
        input_output_aliases={9: 0, 10: 1, 11: 2, 12: 3},
        compiler_params=pltpu.CompilerParams(
            dimension_semantics=("parallel",), vmem_limit_bytes=VMEM_LIMIT),
        name="post_attn_meta",
    )(att_t, convn, meta_tokens, *weights, *real)


def _moe_kernel(bexp_ref, nused_ref, tok_ref, dst_ref, xn_hbm, wgu_ref, bgu_ref, wd_ref, bd_ref,
                ys_hbm, xbuf, ybuf, wgu_bf, wd_bf, sems):
    i = pl.program_id(0)
    rows = xbuf.shape[0]
    ff = wd_bf.shape[0]

    def gather(r):
        return pltpu.make_async_copy(xn_hbm.at[pl.ds(tok_ref[0, 0, r], 1), :], xbuf.at[pl.ds(r, 1), :], sems.at[0])

    def scatter(r):
        return pltpu.make_async_copy(ybuf.at[pl.ds(r, 1), :], ys_hbm.at[pl.ds(dst_ref[0, 0, r], 1), :], sems.at[1])

    @pl.when(i < nused_ref[0])
    def _():
        @pl.loop(0, rows)
        def _(r):
            gather(r).start()

        @pl.when((i == 0) | (bexp_ref[i] != bexp_ref[jnp.maximum(i - 1, 0)]))
        def _():
            wgu_bf[...] = wgu_ref[0].astype(BF16)
            wd_bf[...] = wd_ref[0].astype(BF16)

        @pl.loop(0, rows)
        def _(r):
            gather(r).wait()

        gu = jnp.dot(xbuf[...].astype(BF16), wgu_bf[...], preferred_element_type=F32) + bgu_ref[0]
        gate = jnp.minimum(gu[:, :ff], SWIGLU_LIMIT)
        lin = jnp.clip(gu[:, ff:], -SWIGLU_LIMIT, SWIGLU_LIMIT)
        hid = (lin + 1.0) * (gate * jax.nn.sigmoid(SWIGLU_ALPHA * gate))
        ybuf[...] = jnp.dot(hid.astype(BF16), wd_bf[...], preferred_element_type=F32) + bd_ref[0]

        @pl.loop(0, rows)
        def _(r):
            scatter(r).start()

        @pl.loop(0, rows)
        def _(r):
            scatter(r).wait()


def _moe(xn, block_expert, n_used, slot_tok, slot_dst, w_gu, b_gu, w_down, b_down):
    ntok, d = xn.shape
    e, _, ff2 = w_gu.shape
    ff = ff2 // 2
    nblocks = block_expert.shape[0]
    grid_spec = pltpu.PrefetchScalarGridSpec(
        num_scalar_prefetch=2,
        grid=(nblocks,),
        in_specs=[
            pl.BlockSpec((1, 1, MOE_ROWS), lambda i, be, nu: (i, 0, 0), memory_space=pltpu.SMEM),
            pl.BlockSpec((1, 1, MOE_ROWS), lambda i, be, nu: (i, 0, 0), memory_space=pltpu.SMEM),
            pl.BlockSpec(memory_space=pl.ANY),
            pl.BlockSpec((1, d, ff2), lambda i, be, nu: (be[i], 0, 0)),
            pl.BlockSpec((1, 1, ff2), lambda i, be, nu: (be[i], 0, 0)),
            pl.BlockSpec((1, ff, d), lambda i, be, nu: (be[i], 0, 0)),
            pl.BlockSpec((1, 1, d), lambda i, be, nu: (be[i], 0, 0)),
        ],
        out_specs=pl.BlockSpec(memory_space=pl.ANY),
        scratch_shapes=[
            pltpu.VMEM((MOE_ROWS, d), F32),
            pltpu.VMEM((MOE_ROWS, d), F32),
            pltpu.VMEM((d, ff2), BF16),
            pltpu.VMEM((ff, d), BF16),
            pltpu.SemaphoreType.DMA((2,)),
        ],
    )
    return pl.pallas_call(
        _moe_kernel,
        grid_spec=grid_spec,
        out_shape=jax.ShapeDtypeStruct(((TOP_K + 1) * ntok, d), F32),
        compiler_params=pltpu.CompilerParams(
            dimension_semantics=("arbitrary",), vmem_limit_bytes=VMEM_LIMIT),
        name="moe_experts",
    )(block_expert, n_used, slot_tok.reshape(nblocks, 1, MOE_ROWS), slot_dst.reshape(nblocks, 1, MOE_ROWS),
      xn, w_gu, b_gu.reshape(e, 1, ff2), w_down, b_down.reshape(e, 1, d))


def _route(top_e, n_experts):
    ntok = top_e.shape[0]
    npairs = ntok * TOP_K
    e_flat = top_e.reshape(-1)
    order = jnp.argsort(e_flat).astype(jnp.int32)
    e_sorted = e_flat[order]
    counts = jnp.zeros((n_experts,), jnp.int32).at[e_flat].add(1)
    padded = (counts + MOE_ROWS - 1) // MOE_ROWS * MOE_ROWS
    start = jnp.cumsum(counts) - counts
    pend = jnp.cumsum(padded)
    pstart = pend - padded
    dest = pstart[e_sorted] + (jnp.arange(npairs, dtype=jnp.int32) - start[e_sorted])
    nblocks = -(-npairs // MOE_ROWS) + n_experts
    cap = nblocks * MOE_ROWS
    slot_src = jnp.full((cap,), -1, jnp.int32).at[dest].set(order)
    empty = slot_src < 0
    pair = jnp.maximum(slot_src, 0)
    slot_tok = jnp.where(empty, 0, pair // TOP_K)
    spare = npairs + jnp.arange(cap, dtype=jnp.int32) % MOE_ROWS
    slot_dst = jnp.where(empty, spare, (pair % TOP_K) * ntok + pair // TOP_K)
    block_start = jnp.arange(nblocks, dtype=jnp.int32) * MOE_ROWS
    block_expert = jnp.clip(jnp.searchsorted(pend, block_start, side="right"), 0, n_experts - 1).astype(jnp.int32)
    n_used = (pend[-1] // MOE_ROWS).astype(jnp.int32).reshape(1)
    return block_expert, n_used, slot_tok.astype(jnp.int32), slot_dst.astype(jnp.int32)


def _combine_kernel(h1_ref, gt_ref, y0_ref, y1_ref, y2_ref, y3_ref, g_ref, o_ref):
    gates = gt_ref[...]
    acc = h1_ref[...]
    for kk, y_ref in enumerate((y0_ref, y1_ref, y2_ref, y3_ref)):
        acc = acc + gates[:, kk:kk + 1] * y_ref[0]
    o_ref[...] = _rms(acc, g_ref[...])


def _combine(h1, gates, ys, g_final, nreal):
    ntok, d = h1.shape
    ys4 = ys.reshape(TOP_K + 1, ntok, d)
    yspec = lambda kk: pl.BlockSpec((1, ROW_BLOCK, d), lambda j, kk=kk: (kk, j, 0))
    return pl.pallas_call(
        _combine_kernel,
        grid=(nreal // ROW_BLOCK,),
        in_specs=[pl.BlockSpec((ROW_BLOCK, d), lambda j: (j, 0)),
                  pl.BlockSpec((ROW_BLOCK, TOP_K), lambda j: (j, 0)),
                  yspec(0), yspec(1), yspec(2), yspec(3),
                  pl.BlockSpec((1, d), lambda j: (0, 0))],
        out_specs=pl.BlockSpec((ROW_BLOCK, d), lambda j: (j, 0)),
        out_shape=jax.ShapeDtypeStruct((nreal, d), F32),
        compiler_params=pltpu.CompilerParams(
            dimension_semantics=("parallel",), vmem_limit_bytes=VMEM_LIMIT),
        name="moe_combine",
    )(h1, gates, ys4, ys4, ys4, ys4, g_final)


def kernel(x, meta_tokens, g_mix, w_in, g_attn_out, conv_w, conv_b, g_conv_out, w_out, g_ffn,
           w_router, b_router, w_gu, b_gu, w_down, b_down, g_final):
    b, seq, d = x.shape
    depth = w_in.shape[0]
    assert depth == 1 and seq % ROW_BLOCK == 0
    a = w_out.shape[1] // 2
    cw = w_out.shape[1] - a
    heads = a // HEAD_DIM
    n_experts = w_router.shape[2]
    tp = seq + ROW_BLOCK
    nchunks = tp // KEY_CHUNK

    meta_blk = jnp.concatenate([jnp.zeros((ROW_BLOCK - N_META, d), x.dtype), meta_tokens], axis=0)
    w = w_in[0]
    scale = HEAD_DIM ** -0.5 * LOG2_E
    wqt = (w[:, :a] * scale).T.astype(BF16)
    wk = w[:, a:2 * a].astype(BF16)
    wvt = w[:, 2 * a:3 * a].T.astype(BF16)
    wc = w[:, 3 * a:].astype(BF16)
    qt, k, vt, convn = _in_proj(x, meta_blk, g_mix, wqt, wk, wvt, wc, conv_w[0], conv_b, g_conv_out)

    k5 = (k.reshape(b, nchunks, SUBLANES, KEY_GROUPS, heads, HEAD_DIM)
          .transpose(0, 4, 1, 3, 2, 5).reshape(b, heads, nchunks, KEY_CHUNK, HEAD_DIM))
    v5 = (vt.reshape(b, heads, HEAD_DIM, nchunks, SUBLANES, KEY_GROUPS)
          .transpose(0, 1, 3, 2, 5, 4).reshape(b, heads, nchunks, HEAD_DIM, KEY_CHUNK))
    att_t = _attention(qt, k5, v5)

    wo = w_out[0].astype(BF16)
    h1, xn, top_e, gates = _post(att_t, convn, x, meta_tokens, g_attn_out, wo[:a], wo[a:], g_ffn,
                                 w_router[0], b_router)

    ntok = h1.shape[0]
    block_expert, n_used, slot_tok, slot_dst = _route(top_e, n_experts)
    ys = _moe(xn, block_expert, n_used, slot_tok, slot_dst, w_gu[0], b_gu[0], w_down[0], b_down[0])
    out = _combine(h1, gates, ys, g_final.reshape(1, d), b * seq)
    return out.reshape(b, seq, d)
```

```python
import functools

import jax
import jax.numpy as jnp
from jax import lax
from jax.experimental import pallas as pl
from jax.experimental.pallas import tpu as pltpu

N_META = 16
HEAD_DIM = 64
CONV_K = 3
TOP_K = 4
SWIGLU_LIMIT = 7.0
SWIGLU_ALPHA = 1.702
RMS_EPS = 1e-6

SUBLANES = 8
ROW_BLOCK = 256
KEY_CHUNK = 128
KEY_GROUPS = KEY_CHUNK // SUBLANES
ATTN_CHUNKS_PER_ITER = ROW_BLOCK // KEY_CHUNK
LOG2_E = 1.4426950408889634
MOE_ROWS = 256
TOK_BLOCK = 128
SLAB = 32
CARRY_ROWS = SUBLANES
VMEM_LIMIT = 48 * 1024 * 1024

F32 = jnp.float32
BF16 = jnp.bfloat16


def _rms(x, g):
    return x * lax.rsqrt(jnp.mean(x * x, axis=-1, keepdims=True) + RMS_EPS) * g


def _in_proj_kernel(x_ref, meta_ref, g_ref, wqt_ref, wk_ref, wvt_ref, wc_ref, cw_ref, cb_ref, gc_ref,
                    qt_ref, k_ref, vt_ref, conv_ref, u_scr):
    j = pl.program_id(1)
    rows = x_ref.shape[1]
    cw = conv_ref.shape[2]
    h = jnp.where(j == 0, meta_ref[...], x_ref[0])
    nb = _rms(h, g_ref[...]).astype(BF16)
    nt_dims = (((1,), (1,)), ((), ()))
    qt_ref[0, 0] = lax.dot_general(wqt_ref[...], nb, nt_dims, preferred_element_type=F32).astype(BF16)
    vt_ref[0] = lax.dot_general(wvt_ref[...], nb, nt_dims, preferred_element_type=F32).astype(BF16)
    k_ref[0] = jnp.dot(nb, wk_ref[...], preferred_element_type=F32).astype(BF16)
    cp = jnp.dot(nb, wc_ref[...], preferred_element_type=F32)
    gate_b, u = cp[:, :cw], cp[:, cw:2 * cw] * cp[:, 2 * cw:]

    @pl.when(j == 0)
    def _():
        u_scr[0:CARRY_ROWS, :] = jnp.zeros((CARRY_ROWS, cw), F32)

    u_scr[CARRY_ROWS:CARRY_ROWS + rows, :] = u
    u1 = u_scr[CARRY_ROWS - 1:CARRY_ROWS - 1 + rows, :]
    u2 = u_scr[CARRY_ROWS - 2:CARRY_ROWS - 2 + rows, :]
    y = cw_ref[0:1, :] * u2 + cw_ref[1:2, :] * u1 + cw_ref[2:3, :] * u
    u_scr[0:CARRY_ROWS, :] = u_scr[rows:rows + CARRY_ROWS, :]
    conv = gate_b * (y + cb_ref[...])
    conv_ref[0] = _rms(conv, gc_ref[...]).astype(BF16)


def _in_proj(x, meta_blk, g_mix, wqt, wk, wvt, wc, conv_w, conv_b, g_conv):
    b, seq, d = x.shape
    a = wk.shape[1]
    cw = wc.shape[1] // 3
    nblk = seq // ROW_BLOCK + 1
    tp = nblk * ROW_BLOCK
    full = lambda shp: pl.BlockSpec(shp, lambda bi, j: (0,) * len(shp))
    return pl.pallas_call(
        _in_proj_kernel,
        grid=(b, nblk),
        in_specs=[
            pl.BlockSpec((1, ROW_BLOCK, d), lambda bi, j: (bi, jnp.maximum(j - 1, 0), 0)),
            full((ROW_BLOCK, d)), full((1, d)), full((a, d)), full((d, a)), full((a, d)), full((d, 3 * cw)),
            full((CONV_K, cw)), full((1, cw)), full((1, cw)),
        ],
        out_specs=[
            pl.BlockSpec((1, 1, a, ROW_BLOCK), lambda bi, j: (bi, j, 0, 0)),
            pl.BlockSpec((1, ROW_BLOCK, a), lambda bi, j: (bi, j, 0)),
            pl.BlockSpec((1, a, ROW_BLOCK), lambda bi, j: (bi, 0, j)),
            pl.BlockSpec((1, ROW_BLOCK, cw), lambda bi, j: (bi, j, 0)),
        ],
        out_shape=[
            jax.ShapeDtypeStruct((b, nblk, a, ROW_BLOCK), BF16),
            jax.ShapeDtypeStruct((b, tp, a), BF16),
            jax.ShapeDtypeStruct((b, a, tp), BF16),
            jax.ShapeDtypeStruct((b, tp, cw), BF16),
        ],
        scratch_shapes=[pltpu.VMEM((CARRY_ROWS + ROW_BLOCK, cw), F32)],
        compiler_params=pltpu.CompilerParams(
            dimension_semantics=("parallel", "arbitrary"), vmem_limit_bytes=VMEM_LIMIT),
        name="in_proj",
    )(x, meta_blk, g_mix, wqt, wk, wvt, wc, conv_w, conv_b, g_conv)


def _attn_kernel(qt_ref, k_ref, vt_ref, o_ref, z_scr, w_scr):
    nblk, hd, nq = qt_ref.shape[1:]
    per_iter = ATTN_CHUNKS_PER_ITER
    shape3 = (KEY_GROUPS, SUBLANES, nq)
    key_in_chunk = (lax.broadcasted_iota(jnp.int32, shape3, 1) * KEY_GROUPS
                    + lax.broadcasted_iota(jnp.int32, shape3, 0))
    lane = lax.broadcasted_iota(jnp.int32, shape3, 2)
    sub = lax.broadcasted_iota(jnp.int32, (SUBLANES, nq), 0)
    step_top = lambda i, t: (i + 1 - t) * per_iter - 1

    def scores(i, jtop):
        qt = qt_ref[0, i]
        for c in range(per_iter):
            z_scr[c] = jnp.dot(k_ref[0, 0, jtop - c], qt, preferred_element_type=F32)

    def weights(carry, i, jtop, masked):
        for c in range(per_iter):
            z = z_scr[c].reshape(shape3)
            sp = jnp.maximum(z, jnp.log(1.0 + jnp.exp2(jnp.minimum(z, 64.0))) * LOG2_E)
            if masked:
                valid = (jtop - c) * KEY_CHUNK + key_in_chunk < i * nq + lane
                sp = jnp.where(valid, sp, 0.0)
            runs = [None] * KEY_GROUPS
            run = sp[KEY_GROUPS - 1]
            runs[KEY_GROUPS - 1] = run
            for g in range(KEY_GROUPS - 2, -1, -1):
                run = run + sp[g]
                runs[g] = run
            seg = run
            off = jnp.broadcast_to(carry, (SUBLANES, nq))
            for r in range(1, SUBLANES):
                off = off + jnp.where(sub < r, seg[r:r + 1, :], 0.0)
            w = jnp.exp2(z - (jnp.stack(runs, axis=0) + off[None]))
            if masked:
                w = jnp.where(valid, w, 0.0)
            w_scr[c] = w.reshape(KEY_CHUNK, nq).astype(BF16)
            carry = off[0:1, :] + seg[0:1, :]
        return carry

    def weighted_values(acc, jtop):
        for c in range(per_iter):
            acc = acc + jnp.dot(vt_ref[0, 0, jtop - c], w_scr[c], preferred_element_type=F32)
        return acc

    w_scr[...] = jnp.zeros(w_scr.shape, BF16)
    scores(0, step_top(0, 0))

    def query_block(i, acc):
        acc = weighted_values(acc, per_iter - 1)
        o_ref[0, jnp.maximum(i - 1, 0)] = acc
        carry = weights(jnp.zeros((1, nq), F32), i, step_top(i, 0), True)
        nxt = jnp.minimum(i + 1, nblk - 1)
        scores(jnp.where(i > 0, i, nxt), jnp.where(i > 0, step_top(i, 1), step_top(nxt, 0)))

        def step(t, state):
            carry, acc = state
            acc = weighted_values(acc, step_top(i, t - 1))
            carry = weights(carry, i, step_top(i, t), False)
            scores(jnp.where(t < i, i, nxt), jnp.where(t < i, step_top(i, t + 1), step_top(nxt, 0)))
            return carry, acc

        return lax.fori_loop(1, i + 1, step, (carry, jnp.zeros((hd, nq), F32)))[1]

    acc = lax.fori_loop(0, nblk, query_block, jnp.zeros((hd, nq), F32))
    o_ref[0, nblk - 1] = weighted_values(acc, per_iter - 1)


def _attention(qt, k5, v5):
    b, nblk, a, nq = qt.shape
    heads = a // HEAD_DIM
    nchunks = k5.shape[2]
    return pl.pallas_call(
        _attn_kernel,
        grid=(b, heads),
        in_specs=[
            pl.BlockSpec((1, nblk, HEAD_DIM, nq), lambda bi, h: (bi, 0, h, 0)),
            pl.BlockSpec((1, 1, nchunks, KEY_CHUNK, HEAD_DIM), lambda bi, h: (bi, h, 0, 0, 0)),
            pl.BlockSpec((1, 1, nchunks, HEAD_DIM, KEY_CHUNK), lambda bi, h: (bi, h, 0, 0, 0)),
        ],
        out_specs=pl.BlockSpec((1, nblk, HEAD_DIM, nq), lambda bi, h: (bi, 0, h, 0)),
        out_shape=jax.ShapeDtypeStruct((b, nblk, a, nq), F32),
        scratch_shapes=[pltpu.VMEM((ATTN_CHUNKS_PER_ITER, KEY_CHUNK, nq), F32),
                        pltpu.VMEM((ATTN_CHUNKS_PER_ITER, KEY_CHUNK, nq), BF16)],
        compiler_params=pltpu.CompilerParams(
            dimension_semantics=("parallel", "parallel"), vmem_limit_bytes=VMEM_LIMIT),
        name="stickbreak_attn",
    )(qt, k5, v5)


def _top_k_gates(logits):
    n, e = logits.shape
    lane = lax.broadcasted_iota(jnp.int32, (n, e), 1)
    slot = lax.broadcasted_iota(jnp.int32, (n, TOP_K), 1)
    top_e = jnp.zeros((n, TOP_K), jnp.int32)
    top_l = jnp.zeros((n, TOP_K), F32)
    cur = logits
    for kk in range(TOP_K):
        m = jnp.max(cur, axis=-1, keepdims=True)
        idx = jnp.min(jnp.where(cur == m, lane, e), axis=-1, keepdims=True)
        top_e = jnp.where(slot == kk, idx, top_e)
        top_l = jnp.where(slot == kk, m, top_l)
        cur = jnp.where(lane == idx, -jnp.inf, cur)
    ex = jnp.exp(top_l - top_l[:, 0:1])
    return top_e, ex / jnp.sum(ex, axis=-1, keepdims=True)


def _post_kernel(at_ref, conv_ref, h_ref, ga_ref, wa_ref, wcv_ref, gf_ref, wr_ref, br_ref, *rest, meta):
    h1_ref, xn_ref, te_ref, gt_ref = rest[-4:]
    att = at_ref[0, 0].T
    convn = conv_ref[0]
    h = h_ref[...] if meta else h_ref[0]
    if meta:
        att, convn = att[ROW_BLOCK - N_META:, :], convn[ROW_BLOCK - N_META:, :]
    attn = _rms(att, ga_ref[...]).astype(BF16)
    h1 = (h + jnp.dot(attn, wa_ref[...], preferred_element_type=F32)
          + jnp.dot(convn, wcv_ref[...], preferred_element_type=F32))
    xn = _rms(h1, gf_ref[...])
    logits = jnp.dot(xn, wr_ref[...], preferred_element_type=F32, precision=lax.Precision.HIGHEST) + br_ref[...]
    top_e, gates = _top_k_gates(logits)
    h1_ref[...] = h1
    xn_ref[...] = xn.astype(BF16)
    te_ref[...] = top_e
    gt_ref[...] = gates


def _post(att_t, convn, x, meta_tokens, g_attn, w_out_a, w_out_c, g_ffn, w_router, b_router):
    b, seq, d = x.shape
    a = att_t.shape[2]
    cw = convn.shape[2]
    e = w_router.shape[1]
    nreal = b * seq
    ntok = nreal + b * N_META
    nblk = seq // ROW_BLOCK
    out_shape = [
        jax.ShapeDtypeStruct((ntok, d), F32),
        jax.ShapeDtypeStruct((ntok, d), BF16),
        jax.ShapeDtypeStruct((ntok, TOP_K), jnp.int32),
        jax.ShapeDtypeStruct((ntok, TOP_K), F32),
    ]
    weights = (g_attn, w_out_a, w_out_c, g_ffn, w_router, b_router)

    def wspecs(nargs):
        return [pl.BlockSpec(w.shape, lambda *idx, n=w.ndim: (0,) * n) for w in weights]

    def ospecs(rows, index):
        return [pl.BlockSpec((rows, d), index), pl.BlockSpec((rows, d), index),
                pl.BlockSpec((rows, TOP_K), index), pl.BlockSpec((rows, TOP_K), index)]

    real = pl.pallas_call(
        functools.partial(_post_kernel, meta=False),
        grid=(b, nblk),
        in_specs=[
            pl.BlockSpec((1, 1, a, ROW_BLOCK), lambda bi, j: (bi, j + 1, 0, 0)),
            pl.BlockSpec((1, ROW_BLOCK, cw), lambda bi, j: (bi, j + 1, 0)),
            pl.BlockSpec((1, ROW_BLOCK, d), lambda bi, j: (bi, j, 0)),
        ] + wspecs(2),
        out_specs=ospecs(ROW_BLOCK, lambda bi, j: (bi * nblk + j, 0)),
        out_shape=out_shape,
        compiler_params=pltpu.CompilerParams(
            dimension_semantics=("parallel", "parallel"), vmem_limit_bytes=VMEM_LIMIT),
        name="post_attn_real",
    )(att_t, convn, x, *weights)

    meta_blk0 = nreal // N_META
    return pl.pallas_call(
        functools.partial(_post_kernel, meta=True),
        grid=(b,),
        in_specs=[
            pl.BlockSpec((1, 1, a, ROW_BLOCK), lambda bi: (bi, 0, 0, 0)),
            pl.BlockSpec((1, ROW_BLOCK, cw), lambda bi: (bi, 0, 0)),
            pl.BlockSpec((N_META, d), lambda bi: (0, 0)),
        ] + wspecs(1) + [pl.BlockSpec(memory_space=pl.ANY)] * 4,
        out_specs=ospecs(N_META, lambda bi: (meta_blk0 + bi, 0)),
        out_shape=out_shape,
        input_output_aliases={9: 0, 10: 1, 11: 2, 12: 3},
        compiler_params=pltpu.CompilerParams(
            dimension_semantics=("parallel",), vmem_limit_bytes=VMEM_LIMIT),
        name="post_attn_meta",
    )(att_t, convn, meta_tokens, *weights, *real)


def _rank_one_hot(te, n_experts, shift):
    t = te.shape[1]
    eio = lax.broadcasted_iota(jnp.int32, (n_experts, t), 0)
    ind = te[0:1, :] == eio
    for kk in range(1, TOP_K):
        ind = ind | (te[kk:kk + 1, :] == eio)
    before = (lax.broadcasted_iota(jnp.int32, (t, t), 0) < lax.broadcasted_iota(jnp.int32, (t, t), 1)).astype(BF16)
    rank = jnp.dot(ind.astype(BF16), before, preferred_element_type=F32)
    rank = jnp.where(ind, rank - shift, -1.0)
    rio = lax.broadcasted_iota(jnp.int32, (SLAB, t), 0).astype(F32)
    sel = [jnp.broadcast_to(rank[e:e + 1, :], (SLAB, t)) == rio for e in range(n_experts)]
    return jnp.concatenate(sel, axis=0), ind


def _dispatch_kernel(base_ref, cnt_ref, npass_ref, rend_ref, te_ref, xn_ref, xs_hbm, slab, zbuf, sem, pend):
    b = pl.program_id(0)
    n_experts = rend_ref.shape[0]
    lt = xn_ref.shape[1] // 128
    srows = SLAB * lt
    slot = b % 2

    def slab_copy(sl, e, p):
        dst = pl.multiple_of((base_ref[b * n_experts + e] + p * SLAB) * lt, lt)
        return pltpu.make_async_copy(slab.at[sl, pl.ds(e * srows, srows), :], xs_hbm.at[pl.ds(dst, srows), :],
                                     sem.at[sl])

    def wait_slot(sl):
        @pl.loop(0, pend[sl])
        def _(_):
            slab_copy(sl, 0, 0).wait()
        pend[sl] = 0

    @pl.when(b == 0)
    def _():
        zbuf[...] = jnp.zeros(zbuf.shape, F32)
        zrows = zbuf.shape[0]

        def clear(e):
            dst = pl.multiple_of(jnp.maximum(rend_ref[e] * lt - zrows, 0), lt)
            return pltpu.make_async_copy(zbuf, xs_hbm.at[pl.ds(dst, zrows), :], sem.at[0])

        for e in range(n_experts):
            clear(e).start()
        for e in range(n_experts):
            clear(e).wait()
        pend[0] = 0
        pend[1] = 0

    def fill(sl, p):
        sel, _ = _rank_one_hot(te_ref[...], n_experts, (p * SLAB).astype(F32))
        rows = jnp.dot(sel.astype(BF16), xn_ref[...], preferred_element_type=F32)
        for c in range(lt):
            slab[sl, pl.ds(c, n_experts * SLAB, stride=lt), :] = rows[:, c * 128:(c + 1) * 128]

    fill(slot, jnp.int32(0))
    wait_slot(1 - slot)
    for e in range(n_experts):
        slab_copy(slot, e, 0).start()
    pend[slot] = n_experts

    @pl.loop(1, npass_ref[b])
    def _(p):
        wait_slot(slot)
        fill(slot, p)
        for e in range(n_experts):
            @pl.when(cnt_ref[b * n_experts + e] > p * SLAB)
            def _():
                cp = slab_copy(slot, e, p)
                cp.start()
                cp.wait()

    @pl.when(b == pl.num_programs(0) - 1)
    def _():
        wait_slot(slot)
        wait_slot(1 - slot)


def _dispatch(top_et, xn, base, cnt, npass, rend, cap):
    ntok, d = xn.shape
    n_experts = rend.shape[0]
    lt = d // 128
    assert d % 128 == 0 and lt % SUBLANES == 0 and ntok % TOK_BLOCK == 0
    grid_spec = pltpu.PrefetchScalarGridSpec(
        num_scalar_prefetch=4,
        grid=(ntok // TOK_BLOCK,),
        in_specs=[pl.BlockSpec((TOP_K, TOK_BLOCK), lambda b, *_: (0, b)),
                  pl.BlockSpec((TOK_BLOCK, d), lambda b, *_: (b, 0))],
        out_specs=pl.BlockSpec(memory_space=pl.ANY),
        scratch_shapes=[pltpu.VMEM((2, n_experts * SLAB * lt, 128), F32),
                        pltpu.VMEM((2 * MOE_ROWS * lt, 128), F32),
                        pltpu.SemaphoreType.DMA((2,)),
                        pltpu.SMEM((2,), jnp.int32)],
    )
    return pl.pallas_call(
        _dispatch_kernel,
        grid_spec=grid_spec,
        out_shape=jax.ShapeDtypeStruct((cap * lt, 128), F32),
        compiler_params=pltpu.CompilerParams(dimension_semantics=("arbitrary",), vmem_limit_bytes=VMEM_LIMIT),
        name="moe_dispatch",
    )(base, cnt, npass, rend, top_et, xn)


def _ffn_kernel(bexp_ref, nused_ref, xs_ref, wgu_ref, bgu_ref, wd_ref, bd_ref, ys_ref, wgu_bf, wd_bf):
    i = pl.program_id(0)
    ff, d = wd_bf.shape
    lt = d // 128

    @pl.when(i < nused_ref[0])
    def _():
        @pl.when((i == 0) | (bexp_ref[i] != bexp_ref[jnp.maximum(i - 1, 0)]))
        def _():
            wgu_bf[...] = wgu_ref[0].astype(BF16)
            wd_bf[...] = wd_ref[0].astype(BF16)

        x = jnp.concatenate([xs_ref[pl.ds(c, MOE_ROWS, stride=lt), :] for c in range(lt)], axis=1).astype(BF16)
        gu = jnp.dot(x, wgu_bf[...], preferred_element_type=F32) + bgu_ref[0]
        gate = jnp.minimum(gu[:, :ff], SWIGLU_LIMIT)
        lin = jnp.clip(gu[:, ff:], -SWIGLU_LIMIT, SWIGLU_LIMIT)
        hid = (lin + 1.0) * (gate * jax.nn.sigmoid(SWIGLU_ALPHA * gate))
        y = jnp.dot(hid.astype(BF16), wd_bf[...], preferred_element_type=F32) + bd_ref[0]
        for c in range(lt):
            ys_ref[pl.ds(c, MOE_ROWS, stride=lt), :] = y[:, c * 128:(c + 1) * 128]

    @pl.when(i >= nused_ref[0])
    def _():
        ys_ref[...] = jnp.zeros(ys_ref.shape, F32)


def _ffn(xs, block_expert, n_used, w_gu, b_gu, w_down, b_down):
    e, d, ff2 = w_gu.shape
    ff = ff2 // 2
    lt = d // 128
    nblocks = block_expert.shape[0]
    rows = MOE_ROWS * lt
    grid_spec = pltpu.PrefetchScalarGridSpec(
        num_scalar_prefetch=2,
        grid=(nblocks,),
        in_specs=[
            pl.BlockSpec((rows, 128), lambda i, be, nu: (i, 0)),
            pl.BlockSpec((1, d, ff2), lambda i, be, nu: (be[i], 0, 0)),
            pl.BlockSpec((1, 1, ff2), lambda i, be, nu: (be[i], 0, 0)),
            pl.BlockSpec((1, ff, d), lambda i, be, nu: (be[i], 0, 0)),
            pl.BlockSpec((1, 1, d), lambda i, be, nu: (be[i], 0, 0)),
        ],
        out_specs=pl.BlockSpec((rows, 128), lambda i, be, nu: (i, 0)),
        scratch_shapes=[pltpu.VMEM((d, ff2), BF16), pltpu.VMEM((ff, d), BF16)],
    )
    return pl.pallas_call(
        _ffn_kernel,
        grid_spec=grid_spec,
        out_shape=jax.ShapeDtypeStruct(xs.shape, F32),
        compiler_params=pltpu.CompilerParams(dimension_semantics=("arbitrary",), vmem_limit_bytes=VMEM_LIMIT),
        name="moe_experts",
    )(block_expert, n_used, xs, w_gu, b_gu.reshape(e, 1, ff2), w_down, b_down.reshape(e, 1, d))


def _route_tables(top_e, n_experts):
    ntok = top_e.shape[0]
    nb = ntok // TOK_BLOCK
    ids = jnp.arange(n_experts, dtype=jnp.int32)
    cnt = jnp.sum(top_e.reshape(nb, TOK_BLOCK * TOP_K, 1) == ids, axis=1, dtype=jnp.int32)
    total = jnp.sum(cnt, axis=0)
    region = (total + SLAB + MOE_ROWS - 1) // MOE_ROWS * MOE_ROWS
    rend = jnp.cumsum(region)
    base = (rend - region)[None, :] + jnp.cumsum(cnt, axis=0) - cnt
    npass = jnp.maximum((jnp.max(cnt, axis=1) + SLAB - 1) // SLAB, 1)
    nblocks = -(-(ntok * TOP_K + n_experts * (SLAB + MOE_ROWS - 1)) // MOE_ROWS)
    block_start = jnp.arange(nblocks, dtype=jnp.int32) * MOE_ROWS
    block_expert = jnp.clip(jnp.searchsorted(rend, block_start, side="right"), 0, n_experts - 1)
    n_used = (rend[-1] // MOE_ROWS).reshape(1)
    i32 = lambda v: v.astype(jnp.int32)
    return (i32(base.reshape(-1)), i32(cnt.reshape(-1)), i32(npass), i32(rend), i32(block_expert), i32(n_used),
            nblocks * MOE_ROWS)


def _combine_kernel(base_ref, cnt_ref, npass_ref, te_ref, gt_ref, h1_ref, g_ref, ys_hbm, o_ref, ybuf, acc, sem):
    b = pl.program_id(0)
    nb = pl.num_programs(0)
    n_experts = ybuf.shape[1] // (SLAB * (h1_ref.shape[1] // 128))
    lt = h1_ref.shape[1] // 128
    srows = SLAB * lt
    slot = b % 2

    def slab_copy(blk, sl, e, p):
        src = pl.multiple_of((base_ref[blk * n_experts + e] + p * SLAB) * lt, lt)
        return pltpu.make_async_copy(ys_hbm.at[pl.ds(src, srows), :], ybuf.at[sl, pl.ds(e * srows, srows), :],
                                     sem.at[sl])

    def fetch(blk, sl):
        for e in range(n_experts):
            slab_copy(blk, sl, e, 0).start()

    @pl.when(b == 0)
    def _():
        fetch(b, slot)

    @pl.when(b + 1 < nb)
    def _():
        fetch(b + 1, 1 - slot)

    for e in range(n_experts):
        slab_copy(b, slot, e, 0).wait()

    def gathered(p):
        te = te_ref[...]
        sel, ind = _rank_one_hot(te, n_experts, (p * SLAB).astype(F32))
        eio = lax.broadcasted_iota(jnp.int32, ind.shape, 0)
        gate_te = jnp.zeros(ind.shape, F32)
        for kk in range(TOP_K):
            gate_te = gate_te + jnp.where(te[kk:kk + 1, :] == eio, gt_ref[kk:kk + 1, :], 0.0)
        gate_rows = jnp.concatenate(
            [jnp.sum(jnp.where(sel[e * SLAB:(e + 1) * SLAB], gate_te[e:e + 1, :], 0.0), axis=1, keepdims=True)
             for e in range(n_experts)], axis=0)
        y = jnp.concatenate([ybuf[slot, pl.ds(c, n_experts * SLAB, stride=lt), :] for c in range(lt)], axis=1)
        y = y * gate_rows
        hi = y.astype(BF16)
        lo = (y - hi.astype(F32)).astype(BF16)
        tn_dims = (((0,), (0,)), ((), ()))
        onehot = sel.astype(BF16)
        return (lax.dot_general(onehot, hi, tn_dims, preferred_element_type=F32)
                + lax.dot_general(onehot, lo, tn_dims, preferred_element_type=F32))

    acc[...] = h1_ref[...] + gathered(jnp.int32(0))

    @pl.loop(1, npass_ref[b])
    def _(p):
        for e in range(n_experts):
            @pl.when(cnt_ref[b * n_experts + e] > p * SLAB)
            def _():
                cp = slab_copy(b, slot, e, p)
                cp.start()
                cp.wait()
        acc[...] += gathered(p)

    o_ref[...] = _rms(acc[...], g_ref[...])


def _combine(h1, top_et, gates_t, ys, base, cnt, npass, g_final, nreal, n_experts):
    d = h1.shape[1]
    lt = d // 128
    grid_spec = pltpu.PrefetchScalarGridSpec(
        num_scalar_prefetch=3,
        grid=(nreal // TOK_BLOCK,),
        in_specs=[pl.BlockSpec((TOP_K, TOK_BLOCK), lambda b, *_: (0, b)),
                  pl.BlockSpec((TOP_K, TOK_BLOCK), lambda b, *_: (0, b)),
                  pl.BlockSpec((TOK_BLOCK, d), lambda b, *_: (b, 0)),
                  pl.BlockSpec((1, d), lambda b, *_: (0, 0)),
                  pl.BlockSpec(memory_space=pl.ANY)],
        out_specs=pl.BlockSpec((TOK_BLOCK, d), lambda b, *_: (b, 0)),
        scratch_shapes=[pltpu.VMEM((2, n_experts * SLAB * lt, 128), F32),
                        pltpu.VMEM((TOK_BLOCK, d), F32),
                        pltpu.SemaphoreType.DMA((2,))],
    )
    return pl.pallas_call(
        _combine_kernel,
        grid_spec=grid_spec,
        out_shape=jax.ShapeDtypeStruct((nreal, d), F32),
        compiler_params=pltpu.CompilerParams(dimension_semantics=("arbitrary",), vmem_limit_bytes=VMEM_LIMIT),
        name="moe_combine",
    )(base, cnt, npass, top_et, gates_t, h1, g_final, ys)


def kernel(x, meta_tokens, g_mix, w_in, g_attn_out, conv_w, conv_b, g_conv_out, w_out, g_ffn,
           w_router, b_router, w_gu, b_gu, w_down, b_down, g_final):
    b, seq, d = x.shape
    depth = w_in.shape[0]
    assert depth == 1 and seq % ROW_BLOCK == 0
    a = w_out.shape[1] // 2
    cw = w_out.shape[1] - a
    heads = a // HEAD_DIM
    n_experts = w_router.shape[2]
    tp = seq + ROW_BLOCK
    nchunks = tp // KEY_CHUNK

    meta_blk = jnp.concatenate([jnp.zeros((ROW_BLOCK - N_META, d), x.dtype), meta_tokens], axis=0)
    w = w_in[0]
    scale = HEAD_DIM ** -0.5 * LOG2_E
    wqt = (w[:, :a] * scale).T.astype(BF16)
    wk = w[:, a:2 * a].astype(BF16)
    wvt = w[:, 2 * a:3 * a].T.astype(BF16)
    wc = w[:, 3 * a:].astype(BF16)
    qt, k, vt, convn = _in_proj(x, meta_blk, g_mix, wqt, wk, wvt, wc, conv_w[0], conv_b, g_conv_out)

    k5 = (k.reshape(b, nchunks, SUBLANES, KEY_GROUPS, heads, HEAD_DIM)
          .transpose(0, 4, 1, 3, 2, 5).reshape(b, heads, nchunks, KEY_CHUNK, HEAD_DIM))
    v5 = (vt.reshape(b, heads, HEAD_DIM, nchunks, SUBLANES, KEY_GROUPS)
          .transpose(0, 1, 3, 2, 5, 4).reshape(b, heads, nchunks, HEAD_DIM, KEY_CHUNK))
    att_t = _attention(qt, k5, v5)

    wo = w_out[0].astype(BF16)
    h1, xn, top_e, gates = _post(att_t, convn, x, meta_tokens, g_attn_out, wo[:a], wo[a:], g_ffn,
                                 w_router[0], b_router)

    base, cnt, npass, rend, block_expert, n_used, cap = _route_tables(top_e, n_experts)
    top_et, gates_t = top_e.T, gates.T
    xs = _dispatch(top_et, xn, base, cnt, npass, rend, cap)
    ys = _ffn(xs, block_expert, n_used, w_gu[0], b_gu[0], w_down[0], b_down[0])
    out = _combine(h1, top_et, gates_t, ys, base, cnt, npass, g_final.reshape(1, d), b * seq, n_experts)
    return out.reshape(b, seq, d)
```

```python
import functools

import jax
import jax.numpy as jnp
from jax import lax
from jax.experimental import pallas as pl
from jax.experimental.pallas import tpu as pltpu

N_META = 16
HEAD_DIM = 64
CONV_K = 3
TOP_K = 4
SWIGLU_LIMIT = 7.0
SWIGLU_ALPHA = 1.702
RMS_EPS = 1e-6

SUBLANES = 8
ROW_BLOCK = 256
KEY_CHUNK = 128
KEY_GROUPS = KEY_CHUNK // SUBLANES
ATTN_CHUNKS_PER_ITER = ROW_BLOCK // KEY_CHUNK
ATTN_HEADS_PER_STEP = 2
LOG2_E = 1.4426950408889634
MOE_ROWS = 256
TOK_BLOCK = 128
SLAB = 32
CARRY_ROWS = SUBLANES
VMEM_LIMIT = 48 * 1024 * 1024

F32 = jnp.float32
BF16 = jnp.bfloat16


def _rms(x, g):
    return x * lax.rsqrt(jnp.mean(x * x, axis=-1, keepdims=True) + RMS_EPS) * g


def _in_proj_kernel(x_ref, meta_ref, g_ref, perm_ref, wqt_ref, wk_ref, wvt_ref, wc_ref, cw_ref, cb_ref, gc_ref,
                    qt_ref, k_ref, vt_ref, conv_ref, u_scr):
    j = pl.program_id(1)
    rows = x_ref.shape[1]
    cw = conv_ref.shape[2]
    h = jnp.where(j == 0, meta_ref[...], x_ref[0])
    nb = _rms(h, g_ref[...]).astype(BF16)
    nt_dims = (((1,), (1,)), ((), ()))
    qt_ref[0, 0] = lax.dot_general(wqt_ref[...], nb, nt_dims, preferred_element_type=F32).astype(BF16)
    nperm = jnp.dot(perm_ref[...], nb, preferred_element_type=F32).astype(BF16)
    kp = jnp.dot(nperm, wk_ref[...], preferred_element_type=F32).astype(BF16)
    vtp = lax.dot_general(wvt_ref[...], nperm, nt_dims, preferred_element_type=F32).astype(BF16)
    for hh in range(k_ref.shape[1]):
        for c in range(rows // KEY_CHUNK):
            k_ref[0, hh, c] = kp[c * KEY_CHUNK:(c + 1) * KEY_CHUNK, hh * HEAD_DIM:(hh + 1) * HEAD_DIM]
            vt_ref[0, hh, c] = vtp[hh * HEAD_DIM:(hh + 1) * HEAD_DIM, c * KEY_CHUNK:(c + 1) * KEY_CHUNK]
    cp = jnp.dot(nb, wc_ref[...], preferred_element_type=F32)
    gate_b, u = cp[:, :cw], cp[:, cw:2 * cw] * cp[:, 2 * cw:]

    @pl.when(j == 0)
    def _():
        u_scr[0:CARRY_ROWS, :] = jnp.zeros((CARRY_ROWS, cw), F32)

    u_scr[CARRY_ROWS:CARRY_ROWS + rows, :] = u
    u1 = u_scr[CARRY_ROWS - 1:CARRY_ROWS - 1 + rows, :]
    u2 = u_scr[CARRY_ROWS - 2:CARRY_ROWS - 2 + rows, :]
    y = cw_ref[0:1, :] * u2 + cw_ref[1:2, :] * u1 + cw_ref[2:3, :] * u
    u_scr[0:CARRY_ROWS, :] = u_scr[rows:rows + CARRY_ROWS, :]
    conv = gate_b * (y + cb_ref[...])
    conv_ref[0] = _rms(conv, gc_ref[...]).astype(BF16)


def _key_order_permutation():
    p = jnp.arange(ROW_BLOCK)
    chunk, g, r = p // KEY_CHUNK, (p % KEY_CHUNK) // SUBLANES, p % SUBLANES
    src = chunk * KEY_CHUNK + r * KEY_GROUPS + g
    return (src[:, None] == jnp.arange(ROW_BLOCK)[None, :]).astype(BF16)


def _in_proj(x, meta_blk, g_mix, wqt, wk, wvt, wc, conv_w, conv_b, g_conv):
    b, seq, d = x.shape
    a = wk.shape[1]
    cw = wc.shape[1] // 3
    heads = a // HEAD_DIM
    nblk = seq // ROW_BLOCK + 1
    tp = nblk * ROW_BLOCK
    cpb = ROW_BLOCK // KEY_CHUNK
    full = lambda shp: pl.BlockSpec(shp, lambda bi, j: (0,) * len(shp))
    return pl.pallas_call(
        _in_proj_kernel,
        grid=(b, nblk),
        in_specs=[
            pl.BlockSpec((1, ROW_BLOCK, d), lambda bi, j: (bi, jnp.maximum(j - 1, 0), 0)),
            full((ROW_BLOCK, d)), full((1, d)), full((ROW_BLOCK, ROW_BLOCK)),
            full((a, d)), full((d, a)), full((a, d)), full((d, 3 * cw)),
            full((CONV_K, cw)), full((1, cw)), full((1, cw)),
        ],
        out_specs=[
            pl.BlockSpec((1, 1, a, ROW_BLOCK), lambda bi, j: (bi, j, 0, 0)),
            pl.BlockSpec((1, heads, cpb, KEY_CHUNK, HEAD_DIM), lambda bi, j: (bi, 0, j, 0, 0)),
            pl.BlockSpec((1, heads, cpb, HEAD_DIM, KEY_CHUNK), lambda bi, j: (bi, 0, j, 0, 0)),
            pl.BlockSpec((1, ROW_BLOCK, cw), lambda bi, j: (bi, j, 0)),
        ],
        out_shape=[
            jax.ShapeDtypeStruct((b, nblk, a, ROW_BLOCK), BF16),
            jax.ShapeDtypeStruct((b, heads, tp // KEY_CHUNK, KEY_CHUNK, HEAD_DIM), BF16),
            jax.ShapeDtypeStruct((b, heads, tp // KEY_CHUNK, HEAD_DIM, KEY_CHUNK), BF16),
            jax.ShapeDtypeStruct((b, tp, cw), BF16),
        ],
        scratch_shapes=[pltpu.VMEM((CARRY_ROWS + ROW_BLOCK, cw), F32)],
        compiler_params=pltpu.CompilerParams(
            dimension_semantics=("parallel", "arbitrary"), vmem_limit_bytes=VMEM_LIMIT),
        name="in_proj",
    )(x, meta_blk, g_mix, _key_order_permutation(), wqt, wk, wvt, wc, conv_w, conv_b, g_conv)


def _attn_kernel(qt_ref, k_ref, vt_ref, o_ref, z_scr, w_scr):
    nblk, nq = qt_ref.shape[1], qt_ref.shape[3]
    hd = HEAD_DIM
    heads = range(ATTN_HEADS_PER_STEP)
    per_iter = ATTN_CHUNKS_PER_ITER
    shape3 = (KEY_GROUPS, SUBLANES, nq)
    key_in_chunk = (lax.broadcasted_iota(jnp.int32, shape3, 1) * KEY_GROUPS
                    + lax.broadcasted_iota(jnp.int32, shape3, 0))
    lane = lax.broadcasted_iota(jnp.int32, shape3, 2)
    sub = lax.broadcasted_iota(jnp.int32, (SUBLANES, nq), 0)
    step_top = lambda i, t: (i + 1 - t) * per_iter - 1

    def scores(i, jtop):
        for h in heads:
            qt = qt_ref[0, i, h * hd:(h + 1) * hd, :]
            for c in range(per_iter):
                z_scr[h, c] = jnp.dot(k_ref[0, h, jtop - c], qt, preferred_element_type=F32)

    def weights(carries, i, jtop, masked):
        return tuple(head_weights(h, carries[h], i, jtop, masked) for h in heads)

    def head_weights(h, carry, i, jtop, masked):
        for c in range(per_iter):
            z = z_scr[h, c].reshape(shape3)
            sp = jnp.maximum(z, jnp.log(1.0 + jnp.exp2(jnp.minimum(z, 64.0))) * LOG2_E)
            if masked:
                valid = (jtop - c) * KEY_CHUNK + key_in_chunk < i * nq + lane
                sp = jnp.where(valid, sp, 0.0)
            runs = [None] * KEY_GROUPS
            run = sp[KEY_GROUPS - 1]
            runs[KEY_GROUPS - 1] = run
            for g in range(KEY_GROUPS - 2, -1, -1):
                run = run + sp[g]
                runs[g] = run
            tail = run
            for shift in (1, 2, 4):
                tail = tail + jnp.where(sub < SUBLANES - shift, pltpu.roll(tail, SUBLANES - shift, 0), 0.0)
            off = carry + (tail - run)
            w = jnp.exp2(z - (jnp.stack(runs, axis=0) + off[None]))
            if masked:
                w = jnp.where(valid, w, 0.0)
            w_scr[h, c] = w.reshape(KEY_CHUNK, nq).astype(BF16)
            carry = carry + tail[0:1, :]
        return carry

    def weighted_values(accs, jtop):
        accs = list(accs)
        for h in heads:
            for c in range(per_iter):
                accs[h] = accs[h] + jnp.dot(vt_ref[0, h, jtop - c], w_scr[h, c], preferred_element_type=F32)
        return tuple(accs)

    def store(i, accs):
        for h in heads:
            o_ref[0, i, h * hd:(h + 1) * hd, :] = accs[h]

    zero_carry = tuple(jnp.zeros((1, nq), F32) for _ in heads)
    zero_acc = tuple(jnp.zeros((hd, nq), F32) for _ in heads)
    w_scr[...] = jnp.zeros(w_scr.shape, BF16)
    scores(0, step_top(0, 0))

    def query_block(i, accs):
        store(jnp.maximum(i - 1, 0), weighted_values(accs, per_iter - 1))
        carries = weights(zero_carry, i, step_top(i, 0), True)
        nxt = jnp.minimum(i + 1, nblk - 1)
        scores(jnp.where(i > 0, i, nxt), jnp.where(i > 0, step_top(i, 1), step_top(nxt, 0)))

        def step(t, state):
            carries, accs = state
            accs = weighted_values(accs, step_top(i, t - 1))
            carries = weights(carries, i, step_top(i, t), False)
            scores(jnp.where(t < i, i, nxt), jnp.where(t < i, step_top(i, t + 1), step_top(nxt, 0)))
            return carries, accs

        return lax.fori_loop(1, i + 1, step, (carries, zero_acc))[1]

    accs = lax.fori_loop(0, nblk, query_block, zero_acc)
    store(nblk - 1, weighted_values(accs, per_iter - 1))


def _attention(qt, k5, v5):
    b, nblk, a, nq = qt.shape
    heads = a // HEAD_DIM
    nchunks = k5.shape[2]
    hps = ATTN_HEADS_PER_STEP
    return pl.pallas_call(
        _attn_kernel,
        grid=(b, heads // hps),
        in_specs=[
            pl.BlockSpec((1, nblk, hps * HEAD_DIM, nq), lambda bi, h: (bi, 0, h, 0)),
            pl.BlockSpec((1, hps, nchunks, KEY_CHUNK, HEAD_DIM), lambda bi, h: (bi, h, 0, 0, 0)),
            pl.BlockSpec((1, hps, nchunks, HEAD_DIM, KEY_CHUNK), lambda bi, h: (bi, h, 0, 0, 0)),
        ],
        out_specs=pl.BlockSpec((1, nblk, hps * HEAD_DIM, nq), lambda bi, h: (bi, 0, h, 0)),
        out_shape=jax.ShapeDtypeStruct((b, nblk, a, nq), F32),
        scratch_shapes=[pltpu.VMEM((hps, ATTN_CHUNKS_PER_ITER, KEY_CHUNK, nq), F32),
                        pltpu.VMEM((hps, ATTN_CHUNKS_PER_ITER, KEY_CHUNK, nq), BF16)],
        compiler_params=pltpu.CompilerParams(
            dimension_semantics=("parallel", "parallel"), vmem_limit_bytes=VMEM_LIMIT),
        name="stickbreak_attn",
    )(qt, k5, v5)


def _top_k_gates(logits):
    n, e = logits.shape
    lane = lax.broadcasted_iota(jnp.int32, (n, e), 1)
    slot = lax.broadcasted_iota(jnp.int32, (n, TOP_K), 1)
    top_e = jnp.zeros((n, TOP_K), jnp.int32)
    top_l = jnp.zeros((n, TOP_K), F32)
    cur = logits
    for kk in range(TOP_K):
        m = jnp.max(cur, axis=-1, keepdims=True)
        idx = jnp.min(jnp.where(cur == m, lane, e), axis=-1, keepdims=True)
        top_e = jnp.where(slot == kk, idx, top_e)
        top_l = jnp.where(slot == kk, m, top_l)
        cur = jnp.where(lane == idx, -jnp.inf, cur)
    ex = jnp.exp(top_l - top_l[:, 0:1])
    return top_e, ex / jnp.sum(ex, axis=-1, keepdims=True)


def _post_kernel(at_ref, conv_ref, h_ref, ga_ref, wa_ref, wcv_ref, gf_ref, wr_ref, br_ref, *rest, meta):
    h1_ref, xn_ref, te_ref, gt_ref = rest[-4:]
    att = at_ref[0, 0].T
    convn = conv_ref[0]
    h = h_ref[...] if meta else h_ref[0]
    if meta:
        att, convn = att[ROW_BLOCK - N_META:, :], convn[ROW_BLOCK - N_META:, :]
    attn = _rms(att, ga_ref[...]).astype(BF16)
    h1 = (h + jnp.dot(attn, wa_ref[...], preferred_element_type=F32)
          + jnp.dot(convn, wcv_ref[...], preferred_element_type=F32))
    xn = _rms(h1, gf_ref[...])
    logits = jnp.dot(xn, wr_ref[...], preferred_element_type=F32, precision=lax.Precision.HIGHEST) + br_ref[...]
    top_e, gates = _top_k_gates(logits)
    h1_ref[...] = h1
    xn_ref[...] = xn.astype(BF16)
    te_ref[...] = top_e
    gt_ref[...] = gates


def _post(att_t, convn, x, meta_tokens, g_attn, w_out_a, w_out_c, g_ffn, w_router, b_router):
    b, seq, d = x.shape
    a = att_t.shape[2]
    cw = convn.shape[2]
    e = w_router.shape[1]
    nreal = b * seq
    ntok = nreal + b * N_META
    nblk = seq // ROW_BLOCK
    out_shape = [
        jax.ShapeDtypeStruct((ntok, d), F32),
        jax.ShapeDtypeStruct((ntok, d), BF16),
        jax.ShapeDtypeStruct((ntok, TOP_K), jnp.int32),
        jax.ShapeDtypeStruct((ntok, TOP_K), F32),
    ]
    weights = (g_attn, w_out_a, w_out_c, g_ffn, w_router, b_router)

    def wspecs(nargs):
        return [pl.BlockSpec(w.shape, lambda *idx, n=w.ndim: (0,) * n) for w in weights]

    def ospecs(rows, index):
        return [pl.BlockSpec((rows, d), index), pl.BlockSpec((rows, d), index),
                pl.BlockSpec((rows, TOP_K), index), pl.BlockSpec((rows, TOP_K), index)]

    real = pl.pallas_call(
        functools.partial(_post_kernel, meta=False),
        grid=(b, nblk),
        in_specs=[
            pl.BlockSpec((1, 1, a, ROW_BLOCK), lambda bi, j: (bi, j + 1, 0, 0)),
            pl.BlockSpec((1, ROW_BLOCK, cw), lambda bi, j: (bi, j + 1, 0)),
            pl.BlockSpec((1, ROW_BLOCK, d), lambda bi, j: (bi, j, 0)),
        ] + wspecs(2),
        out_specs=ospecs(ROW_BLOCK, lambda bi, j: (bi * nblk + j, 0)),
        out_shape=out_shape,
        compiler_params=pltpu.CompilerParams(
            dimension_semantics=("parallel", "parallel"), vmem_limit_bytes=VMEM_LIMIT),
        name="post_attn_real",
    )(att_t, convn, x, *weights)

    meta_blk0 = nreal // N_META
    return pl.pallas_call(
        functools.partial(_post_kernel, meta=True),
        grid=(b,),
        in_specs=[
            pl.BlockSpec((1, 1, a, ROW_BLOCK), lambda bi: (bi, 0, 0, 0)),
            pl.BlockSpec((1, ROW_BLOCK, cw), lambda bi: (bi, 0, 0)),
            pl.BlockSpec((N_META, d), lambda bi: (0, 0)),
        ] + wspecs(1) + [pl.BlockSpec(memory_space=pl.ANY)] * 4,
        out_specs=ospecs(N_META, lambda bi: (meta_blk0 + bi, 0)),
        out_shape=out_shape,
        input_output_aliases={9: 0, 10: 1, 11: 2, 12: 3},
        compiler_params=pltpu.CompilerParams(
            dimension_semantics=("parallel",), vmem_limit_bytes=VMEM_LIMIT),
        name="post_attn_meta",
    )(att_t, convn, meta_tokens, *weights, *real)


def _rank_one_hot(te, n_experts, shift):
    t = te.shape[1]
    eio = lax.broadcasted_iota(jnp.int32, (n_experts, t), 0)
    ind = te[0:1, :] == eio
    for kk in range(1, TOP_K):
        ind = ind | (te[kk:kk + 1, :] == eio)
    before = (lax.broadcasted_iota(jnp.int32, (t, t), 0) < lax.broadcasted_iota(jnp.int32, (t, t), 1)).astype(BF16)
    rank = jnp.dot(ind.astype(BF16), before, preferred_element_type=F32)
    rank = jnp.where(ind, rank - shift, -1.0)
    rio = lax.broadcasted_iota(jnp.int32, (SLAB, t), 0).astype(F32)
    sel = [jnp.broadcast_to(rank[e:e + 1, :], (SLAB, t)) == rio for e in range(n_experts)]
    return jnp.concatenate(sel, axis=0), ind


def _dispatch_kernel(base_ref, cnt_ref, npass_ref, rend_ref, te_ref, xn_ref, xs_hbm, slab, zbuf, sem, pend):
    b = pl.program_id(0)
    n_experts = rend_ref.shape[0]
    lt = xn_ref.shape[1] // 128
    srows = SLAB * lt
    slot = b % 2

    def slab_copy(sl, e, p):
        dst = pl.multiple_of((base_ref[b * n_experts + e] + p * SLAB) * lt, lt)
        return pltpu.make_async_copy(slab.at[sl, pl.ds(e * srows, srows), :], xs_hbm.at[pl.ds(dst, srows), :],
                                     sem.at[sl])

    def wait_slot(sl):
        @pl.loop(0, pend[sl])
        def _(_):
            slab_copy(sl, 0, 0).wait()
        pend[sl] = 0

    @pl.when(b == 0)
    def _():
        zbuf[...] = jnp.zeros(zbuf.shape, F32)
        zrows = zbuf.shape[0]

        def clear(e):
            dst = pl.multiple_of(jnp.maximum(rend_ref[e] * lt - zrows, 0), lt)
            return pltpu.make_async_copy(zbuf, xs_hbm.at[pl.ds(dst, zrows), :], sem.at[0])

        for e in range(n_experts):
            clear(e).start()
            clear(e).wait()
        pend[0] = 0
        pend[1] = 0

    def fill(sl, p):
        sel, _ = _rank_one_hot(te_ref[...], n_experts, (p * SLAB).astype(F32))
        rows = jnp.dot(sel.astype(BF16), xn_ref[...], preferred_element_type=F32)
        for c in range(lt):
            slab[sl, pl.ds(c, n_experts * SLAB, stride=lt), :] = rows[:, c * 128:(c + 1) * 128]

    fill(slot, jnp.int32(0))
    wait_slot(1 - slot)
    for e in range(n_experts):
        slab_copy(slot, e, 0).start()
    pend[slot] = n_experts

    @pl.loop(1, npass_ref[b])
    def _(p):
        wait_slot(slot)
        fill(slot, p)
        for e in range(n_experts):
            @pl.when(cnt_ref[b * n_experts + e] > p * SLAB)
            def _():
                cp = slab_copy(slot, e, p)
                cp.start()
                cp.wait()

    @pl.when(b == pl.num_programs(0) - 1)
    def _():
        wait_slot(slot)
        wait_slot(1 - slot)


def _dispatch(top_et, xn, base, cnt, npass, rend, cap):
    ntok, d = xn.shape
    n_experts = rend.shape[0]
    lt = d // 128
    assert d % 128 == 0 and lt % SUBLANES == 0 and ntok % TOK_BLOCK == 0
    grid_spec = pltpu.PrefetchScalarGridSpec(
        num_scalar_prefetch=4,
        grid=(ntok // TOK_BLOCK,),
        in_specs=[pl.BlockSpec((TOP_K, TOK_BLOCK), lambda b, *_: (0, b)),
                  pl.BlockSpec((TOK_BLOCK, d), lambda b, *_: (b, 0))],
        out_specs=pl.BlockSpec(memory_space=pl.ANY),
        scratch_shapes=[pltpu.VMEM((2, n_experts * SLAB * lt, 128), F32),
                        pltpu.VMEM((2 * MOE_ROWS * lt, 128), F32),
                        pltpu.SemaphoreType.DMA((2,)),
                        pltpu.SMEM((2,), jnp.int32)],
    )
    return pl.pallas_call(
        _dispatch_kernel,
        grid_spec=grid_spec,
        out_shape=jax.ShapeDtypeStruct((cap * lt, 128), F32),
        compiler_params=pltpu.CompilerParams(dimension_semantics=("arbitrary",), vmem_limit_bytes=VMEM_LIMIT),
        name="moe_dispatch",
    )(base, cnt, npass, rend, top_et, xn)


def _ffn_kernel(bexp_ref, nused_ref, xs_ref, wgu_ref, bgu_ref, wd_ref, bd_ref, ys_ref, wgu_bf, wd_bf):
    i = pl.program_id(0)
    ff, d = wd_bf.shape
    lt = d // 128

    @pl.when(i < nused_ref[0])
    def _():
        @pl.when((i == 0) | (bexp_ref[i] != bexp_ref[jnp.maximum(i - 1, 0)]))
        def _():
            wgu_bf[...] = wgu_ref[0].astype(BF16)
            wd_bf[...] = wd_ref[0].astype(BF16)

        x = jnp.concatenate([xs_ref[pl.ds(c, MOE_ROWS, stride=lt), :] for c in range(lt)], axis=1).astype(BF16)
        gu = jnp.dot(x, wgu_bf[...], preferred_element_type=F32) + bgu_ref[0]
        gate = jnp.minimum(gu[:, :ff], SWIGLU_LIMIT)
        lin = jnp.clip(gu[:, ff:], -SWIGLU_LIMIT, SWIGLU_LIMIT)
        hid = (lin + 1.0) * (gate * jax.nn.sigmoid(SWIGLU_ALPHA * gate))
        y = jnp.dot(hid.astype(BF16), wd_bf[...], preferred_element_type=F32) + bd_ref[0]
        for c in range(lt):
            ys_ref[pl.ds(c, MOE_ROWS, stride=lt), :] = y[:, c * 128:(c + 1) * 128]

    @pl.when(i >= nused_ref[0])
    def _():
        ys_ref[...] = jnp.zeros(ys_ref.shape, F32)


def _ffn(xs, block_expert, n_used, w_gu, b_gu, w_down, b_down):
    e, d, ff2 = w_gu.shape
    ff = ff2 // 2
    lt = d // 128
    nblocks = block_expert.shape[0]
    rows = MOE_ROWS * lt
    grid_spec = pltpu.PrefetchScalarGridSpec(
        num_scalar_prefetch=2,
        grid=(nblocks,),
        in_specs=[
            pl.BlockSpec((rows, 128), lambda i, be, nu: (i, 0)),
            pl.BlockSpec((1, d, ff2), lambda i, be, nu: (be[i], 0, 0)),
            pl.BlockSpec((1, 1, ff2), lambda i, be, nu: (be[i], 0, 0)),
            pl.BlockSpec((1, ff, d), lambda i, be, nu: (be[i], 0, 0)),
            pl.BlockSpec((1, 1, d), lambda i, be, nu: (be[i], 0, 0)),
        ],
        out_specs=pl.BlockSpec((rows, 128), lambda i, be, nu: (i, 0)),
        scratch_shapes=[pltpu.VMEM((d, ff2), BF16), pltpu.VMEM((ff, d), BF16)],
    )
    return pl.pallas_call(
        _ffn_kernel,
        grid_spec=grid_spec,
        out_shape=jax.ShapeDtypeStruct(xs.shape, F32),
        compiler_params=pltpu.CompilerParams(dimension_semantics=("arbitrary",), vmem_limit_bytes=VMEM_LIMIT),
        name="moe_experts",
    )(block_expert, n_used, xs, w_gu, b_gu.reshape(e, 1, ff2), w_down, b_down.reshape(e, 1, d))


def _route_tables(top_e, n_experts):
    ntok = top_e.shape[0]
    nb = ntok // TOK_BLOCK
    ids = jnp.arange(n_experts, dtype=jnp.int32)
    cnt = jnp.sum(top_e.reshape(nb, TOK_BLOCK * TOP_K, 1) == ids, axis=1, dtype=jnp.int32)
    total = jnp.sum(cnt, axis=0)
    region = (total + SLAB + MOE_ROWS - 1) // MOE_ROWS * MOE_ROWS
    rend = jnp.cumsum(region)
    base = (rend - region)[None, :] + jnp.cumsum(cnt, axis=0) - cnt
    npass = jnp.maximum((jnp.max(cnt, axis=1) + SLAB - 1) // SLAB, 1)
    nblocks = -(-(ntok * TOP_K + n_experts * (SLAB + MOE_ROWS - 1)) // MOE_ROWS)
    block_start = jnp.arange(nblocks, dtype=jnp.int32) * MOE_ROWS
    block_expert = jnp.minimum(jnp.sum(rend[None, :] <= block_start[:, None], axis=1), n_experts - 1)
    n_used = (rend[-1] // MOE_ROWS).reshape(1)
    i32 = lambda v: v.astype(jnp.int32)
    return (i32(base.reshape(-1)), i32(cnt.reshape(-1)), i32(npass), i32(rend), i32(block_expert), i32(n_used),
            nblocks * MOE_ROWS)


def _combine_kernel(base_ref, cnt_ref, npass_ref, te_ref, gt_ref, h1_ref, g_ref, ys_hbm, o_ref, ybuf, acc, sem):
    b = pl.program_id(0)
    nb = pl.num_programs(0)
    n_experts = ybuf.shape[1] // (SLAB * (h1_ref.shape[1] // 128))
    lt = h1_ref.shape[1] // 128
    srows = SLAB * lt
    slot = b % 2

    def slab_copy(blk, sl, e, p):
        src = pl.multiple_of((base_ref[blk * n_experts + e] + p * SLAB) * lt, lt)
        return pltpu.make_async_copy(ys_hbm.at[pl.ds(src, srows), :], ybuf.at[sl, pl.ds(e * srows, srows), :],
                                     sem.at[sl])

    def fetch(blk, sl):
        for e in range(n_experts):
            slab_copy(blk, sl, e, 0).start()

    @pl.when(b == 0)
    def _():
        fetch(b, slot)

    @pl.when(b + 1 < nb)
    def _():
        fetch(b + 1, 1 - slot)

    for e in range(n_experts):
        slab_copy(b, slot, e, 0).wait()

    def gathered(p):
        te = te_ref[...]
        sel, ind = _rank_one_hot(te, n_experts, (p * SLAB).astype(F32))
        eio = lax.broadcasted_iota(jnp.int32, ind.shape, 0)
        gate_te = jnp.zeros(ind.shape, F32)
        for kk in range(TOP_K):
            gate_te = gate_te + jnp.where(te[kk:kk + 1, :] == eio, gt_ref[kk:kk + 1, :], 0.0)
        gate_rows = jnp.concatenate(
            [jnp.sum(jnp.where(sel[e * SLAB:(e + 1) * SLAB], gate_te[e:e + 1, :], 0.0), axis=1, keepdims=True)
             for e in range(n_experts)], axis=0)
        y = jnp.concatenate([ybuf[slot, pl.ds(c, n_experts * SLAB, stride=lt), :] for c in range(lt)], axis=1)
        y = y * gate_rows
        hi = y.astype(BF16)
        lo = (y - hi.astype(F32)).astype(BF16)
        tn_dims = (((0,), (0,)), ((), ()))
        onehot = sel.astype(BF16)
        return (lax.dot_general(onehot, hi, tn_dims, preferred_element_type=F32)
                + lax.dot_general(onehot, lo, tn_dims, preferred_element_type=F32))

    acc[...] = h1_ref[...] + gathered(jnp.int32(0))

    @pl.loop(1, npass_ref[b])
    def _(p):
        for e in range(n_experts):
            @pl.when(cnt_ref[b * n_experts + e] > p * SLAB)
            def _():
                cp = slab_copy(b, slot, e, p)
                cp.start()
                cp.wait()
        acc[...] += gathered(p)

    o_ref[...] = _rms(acc[...], g_ref[...])


def _combine(h1, top_et, gates_t, ys, base, cnt, npass, g_final, nreal, n_experts):
    d = h1.shape[1]
    lt = d // 128
    grid_spec = pltpu.PrefetchScalarGridSpec(
        num_scalar_prefetch=3,
        grid=(nreal // TOK_BLOCK,),
        in_specs=[pl.BlockSpec((TOP_K, TOK_BLOCK), lambda b, *_: (0, b)),
                  pl.BlockSpec((TOP_K, TOK_BLOCK), lambda b, *_: (0, b)),
                  pl.BlockSpec((TOK_BLOCK, d), lambda b, *_: (b, 0)),
                  pl.BlockSpec((1, d), lambda b, *_: (0, 0)),
                  pl.BlockSpec(memory_space=pl.ANY)],
        out_specs=pl.BlockSpec((TOK_BLOCK, d), lambda b, *_: (b, 0)),
        scratch_shapes=[pltpu.VMEM((2, n_experts * SLAB * lt, 128), F32),
                        pltpu.VMEM((TOK_BLOCK, d), F32),
                        pltpu.SemaphoreType.DMA((2,))],
    )
    return pl.pallas_call(
        _combine_kernel,
        grid_spec=grid_spec,
        out_shape=jax.ShapeDtypeStruct((nreal, d), F32),
        compiler_params=pltpu.CompilerParams(dimension_semantics=("arbitrary",), vmem_limit_bytes=VMEM_LIMIT),
        name="moe_combine",
    )(base, cnt, npass, top_et, gates_t, h1, g_final, ys)


def kernel(x, meta_tokens, g_mix, w_in, g_attn_out, conv_w, conv_b, g_conv_out, w_out, g_ffn,
           w_router, b_router, w_gu, b_gu, w_down, b_down, g_final):
    b, seq, d = x.shape
    depth = w_in.shape[0]
    assert depth == 1 and seq % ROW_BLOCK == 0
    a = w_out.shape[1] // 2
    n_experts = w_router.shape[2]

    meta_blk = jnp.concatenate([jnp.zeros((ROW_BLOCK - N_META, d), x.dtype), meta_tokens], axis=0)
    w = w_in[0]
    scale = HEAD_DIM ** -0.5 * LOG2_E
    wqt = (w[:, :a] * scale).T.astype(BF16)
    wk = w[:, a:2 * a].astype(BF16)
    wvt = w[:, 2 * a:3 * a].T.astype(BF16)
    wc = w[:, 3 * a:].astype(BF16)
    qt, k5, v5, convn = _in_proj(x, meta_blk, g_mix, wqt, wk, wvt, wc, conv_w[0], conv_b, g_conv_out)
    att_t = _attention(qt, k5, v5)

    wo = w_out[0].astype(BF16)
    h1, xn, top_e, gates = _post(att_t, convn, x, meta_tokens, g_attn_out, wo[:a], wo[a:], g_ffn,
                                 w_router[0], b_router)

    base, cnt, npass, rend, block_expert, n_used, cap = _route_tables(top_e, n_experts)
    top_et, gates_t = top_e.T, gates.T
    xs = _dispatch(top_et, xn, base, cnt, npass, rend, cap)
    ys = _ffn(xs, block_expert, n_used, w_gu[0], b_gu[0], w_down[0], b_down[0])
    out = _combine(h1, top_et, gates_t, ys, base, cnt, npass, g_final.reshape(1, d), b * seq, n_experts)
    return out.reshape(b, seq, d)
```

```python
import functools

import jax
import jax.numpy as jnp
from jax import lax
from jax.experimental import pallas as pl
from jax.experimental.pallas import tpu as pltpu

N_META = 16
HEAD_DIM = 64
CONV_K = 3
TOP_K = 4
SWIGLU_LIMIT = 7.0
SWIGLU_ALPHA = 1.702
RMS_EPS = 1e-6

SUBLANES = 8
ROW_BLOCK = 256
KEY_CHUNK = 128
KEY_GROUPS = KEY_CHUNK // SUBLANES
ATTN_CHUNKS_PER_ITER = ROW_BLOCK // KEY_CHUNK
ATTN_HEADS_PER_STEP = 2
STICK_DEAD_BITS = 160.0
LOG2_E = 1.4426950408889634
MOE_ROWS = 256
TOK_BLOCK = 128
SLAB = 32
CARRY_ROWS = SUBLANES
VMEM_LIMIT = 48 * 1024 * 1024

F32 = jnp.float32
BF16 = jnp.bfloat16


def _rms(x, g):
    return x * lax.rsqrt(jnp.mean(x * x, axis=-1, keepdims=True) + RMS_EPS) * g


def _in_proj_kernel(x_ref, meta_ref, g_ref, perm_ref, wqt_ref, wk_ref, wvt_ref, wc_ref, cw_ref, cb_ref, gc_ref,
                    qt_ref, k_ref, vt_ref, conv_ref, u_scr):
    j = pl.program_id(1)
    rows = x_ref.shape[1]
    cw = conv_ref.shape[2]
    h = jnp.where(j == 0, meta_ref[...], x_ref[0])
    nb = _rms(h, g_ref[...]).astype(BF16)
    nt_dims = (((1,), (1,)), ((), ()))
    qt_ref[0, 0] = lax.dot_general(wqt_ref[...], nb, nt_dims, preferred_element_type=F32).astype(BF16)
    nperm = jnp.dot(perm_ref[...], nb, preferred_element_type=F32).astype(BF16)
    kp = jnp.dot(nperm, wk_ref[...], preferred_element_type=F32).astype(BF16)
    vtp = lax.dot_general(wvt_ref[...], nperm, nt_dims, preferred_element_type=F32).astype(BF16)
    for hh in range(k_ref.shape[1]):
        for c in range(rows // KEY_CHUNK):
            k_ref[0, hh, c] = kp[c * KEY_CHUNK:(c + 1) * KEY_CHUNK, hh * HEAD_DIM:(hh + 1) * HEAD_DIM]
            vt_ref[0, hh, c] = vtp[hh * HEAD_DIM:(hh + 1) * HEAD_DIM, c * KEY_CHUNK:(c + 1) * KEY_CHUNK]
    cp = jnp.dot(nb, wc_ref[...], preferred_element_type=F32)
    gate_b, u = cp[:, :cw], cp[:, cw:2 * cw] * cp[:, 2 * cw:]

    @pl.when(j == 0)
    def _():
        u_scr[0:CARRY_ROWS, :] = jnp.zeros((CARRY_ROWS, cw), F32)

    u_scr[CARRY_ROWS:CARRY_ROWS + rows, :] = u
    u1 = u_scr[CARRY_ROWS - 1:CARRY_ROWS - 1 + rows, :]
    u2 = u_scr[CARRY_ROWS - 2:CARRY_ROWS - 2 + rows, :]
    y = cw_ref[0:1, :] * u2 + cw_ref[1:2, :] * u1 + cw_ref[2:3, :] * u
    u_scr[0:CARRY_ROWS, :] = u_scr[rows:rows + CARRY_ROWS, :]
    conv = gate_b * (y + cb_ref[...])
    conv_ref[0] = _rms(conv, gc_ref[...]).astype(BF16)


def _key_order_permutation():
    p = jnp.arange(ROW_BLOCK)
    chunk, g, r = p // KEY_CHUNK, (p % KEY_CHUNK) // SUBLANES, p % SUBLANES
    src = chunk * KEY_CHUNK + r * KEY_GROUPS + g
    return (src[:, None] == jnp.arange(ROW_BLOCK)[None, :]).astype(BF16)


def _in_proj(x, meta_blk, g_mix, wqt, wk, wvt, wc, conv_w, conv_b, g_conv):
    b, seq, d = x.shape
    a = wk.shape[1]
    cw = wc.shape[1] // 3
    heads = a // HEAD_DIM
    nblk = seq // ROW_BLOCK + 1
    tp = nblk * ROW_BLOCK
    cpb = ROW_BLOCK // KEY_CHUNK
    full = lambda shp: pl.BlockSpec(shp, lambda bi, j: (0,) * len(shp))
    return pl.pallas_call(
        _in_proj_kernel,
        grid=(b, nblk),
        in_specs=[
            pl.BlockSpec((1, ROW_BLOCK, d), lambda bi, j: (bi, jnp.maximum(j - 1, 0), 0)),
            full((ROW_BLOCK, d)), full((1, d)), full((ROW_BLOCK, ROW_BLOCK)),
            full((a, d)), full((d, a)), full((a, d)), full((d, 3 * cw)),
            full((CONV_K, cw)), full((1, cw)), full((1, cw)),
        ],
        out_specs=[
            pl.BlockSpec((1, 1, a, ROW_BLOCK), lambda bi, j: (bi, j, 0, 0)),
            pl.BlockSpec((1, heads, cpb, KEY_CHUNK, HEAD_DIM), lambda bi, j: (bi, 0, j, 0, 0)),
            pl.BlockSpec((1, heads, cpb, HEAD_DIM, KEY_CHUNK), lambda bi, j: (bi, 0, j, 0, 0)),
            pl.BlockSpec((1, ROW_BLOCK, cw), lambda bi, j: (bi, j, 0)),
        ],
        out_shape=[
            jax.ShapeDtypeStruct((b, nblk, a, ROW_BLOCK), BF16),
            jax.ShapeDtypeStruct((b, heads, tp // KEY_CHUNK, KEY_CHUNK, HEAD_DIM), BF16),
            jax.ShapeDtypeStruct((b, heads, tp // KEY_CHUNK, HEAD_DIM, KEY_CHUNK), BF16),
            jax.ShapeDtypeStruct((b, tp, cw), BF16),
        ],
        scratch_shapes=[pltpu.VMEM((CARRY_ROWS + ROW_BLOCK, cw), F32)],
        compiler_params=pltpu.CompilerParams(
            dimension_semantics=("parallel", "arbitrary"), vmem_limit_bytes=VMEM_LIMIT),
        name="in_proj",
    )(x, meta_blk, g_mix, _key_order_permutation(), wqt, wk, wvt, wc, conv_w, conv_b, g_conv)


def _attn_kernel(qt_ref, k_ref, vt_ref, o_ref, z_scr, zd_scr, w_scr):
    nblk, nq = qt_ref.shape[1], qt_ref.shape[3]
    hd = HEAD_DIM
    heads = range(ATTN_HEADS_PER_STEP)
    per_iter = ATTN_CHUNKS_PER_ITER
    shape3 = (KEY_GROUPS, SUBLANES, nq)
    key_in_chunk = (lax.broadcasted_iota(jnp.int32, shape3, 1) * KEY_GROUPS
                    + lax.broadcasted_iota(jnp.int32, shape3, 0))
    lane = lax.broadcasted_iota(jnp.int32, shape3, 2)
    sub = lax.broadcasted_iota(jnp.int32, (SUBLANES, nq), 0)
    step_top = lambda i, t: (i + 1 - t) * per_iter - 1

    def scores(dst, i, jtop):
        for h in heads:
            qt = qt_ref[0, i, h * hd:(h + 1) * hd, :]
            for c in range(per_iter):
                dst[h, c] = jnp.dot(k_ref[0, h, jtop - c], qt, preferred_element_type=F32)

    def weights(src, carries, i, jtop, masked):
        return tuple(head_weights(src, h, carries[h], i, jtop, masked) for h in heads)

    def head_weights(src, h, carry, i, jtop, masked):
        for c in range(per_iter):
            z = src[h, c].reshape(shape3)
            sp = jnp.maximum(z, jnp.log(1.0 + jnp.exp2(jnp.minimum(z, 64.0))) * LOG2_E)
            if masked:
                valid = (jtop - c) * KEY_CHUNK + key_in_chunk < i * nq + lane
                sp = jnp.where(valid, sp, 0.0)
            runs = [None] * KEY_GROUPS
            run = sp[KEY_GROUPS - 1]
            runs[KEY_GROUPS - 1] = run
            for g in range(KEY_GROUPS - 2, -1, -1):
                run = run + sp[g]
                runs[g] = run
            tail = run
            for shift in (1, 2, 4):
                tail = tail + jnp.where(sub < SUBLANES - shift, pltpu.roll(tail, SUBLANES - shift, 0), 0.0)
            off = carry + (tail - run)
            w = jnp.exp2(z - (jnp.stack(runs, axis=0) + off[None]))
            if masked:
                w = jnp.where(valid, w, 0.0)
            w_scr[h, c] = w.reshape(KEY_CHUNK, nq).astype(BF16)
            carry = carry + tail[0:1, :]
        return carry

    def weighted_values(accs, jtop):
        accs = list(accs)
        for h in heads:
            for c in range(per_iter):
                accs[h] = accs[h] + jnp.dot(vt_ref[0, h, jtop - c], w_scr[h, c], preferred_element_type=F32)
        return tuple(accs)

    def store(i, accs):
        for h in heads:
            o_ref[0, i, h * hd:(h + 1) * hd, :] = accs[h]

    zero_carry = tuple(jnp.zeros((1, nq), F32) for _ in heads)
    zero_acc = tuple(jnp.zeros((hd, nq), F32) for _ in heads)
    w_scr[...] = jnp.zeros(w_scr.shape, BF16)
    scores(zd_scr, 0, step_top(0, 0))

    def stick_left(carries):
        m = carries[0]
        for h in heads[1:]:
            m = jnp.minimum(m, carries[h])
        return jnp.min(m) < STICK_DEAD_BITS

    def query_block(i, state):
        accs, pending_top = state
        store(jnp.maximum(i - 1, 0), weighted_values(accs, pending_top))
        carries = weights(zd_scr, zero_carry, i, step_top(i, 0), True)
        nxt = jnp.minimum(i + 1, nblk - 1)
        scores(zd_scr, nxt, step_top(nxt, 0))
        scores(z_scr, i, jnp.maximum(step_top(i, 1), per_iter - 1))

        def step(state):
            t, _, carries, accs = state
            live = stick_left(carries)
            accs = weighted_values(accs, step_top(i, t - 1))
            carries = weights(z_scr, carries, i, step_top(i, t), False)
            scores(z_scr, i, jnp.maximum(step_top(i, t + 1), per_iter - 1))
            return t + 1, live, carries, accs

        t_end, _, _, accs = lax.while_loop(lambda s: (s[0] <= i) & s[1], step,
                                           (jnp.int32(1), jnp.bool_(True), carries, zero_acc))
        return accs, step_top(i, t_end - 1)

    accs, pending_top = lax.fori_loop(0, nblk, query_block, (zero_acc, jnp.int32(per_iter - 1)))
    store(nblk - 1, weighted_values(accs, pending_top))


def _attention(qt, k5, v5):
    b, nblk, a, nq = qt.shape
    heads = a // HEAD_DIM
    nchunks = k5.shape[2]
    hps = ATTN_HEADS_PER_STEP
    return pl.pallas_call(
        _attn_kernel,
        grid=(b, heads // hps),
        in_specs=[
            pl.BlockSpec((1, nblk, hps * HEAD_DIM, nq), lambda bi, h: (bi, 0, h, 0)),
            pl.BlockSpec((1, hps, nchunks, KEY_CHUNK, HEAD_DIM), lambda bi, h: (bi, h, 0, 0, 0)),
            pl.BlockSpec((1, hps, nchunks, HEAD_DIM, KEY_CHUNK), lambda bi, h: (bi, h, 0, 0, 0)),
        ],
        out_specs=pl.BlockSpec((1, nblk, hps * HEAD_DIM, nq), lambda bi, h: (bi, 0, h, 0)),
        out_shape=jax.ShapeDtypeStruct((b, nblk, a, nq), F32),
        scratch_shapes=[pltpu.VMEM((hps, ATTN_CHUNKS_PER_ITER, KEY_CHUNK, nq), F32),
                        pltpu.VMEM((hps, ATTN_CHUNKS_PER_ITER, KEY_CHUNK, nq), F32),
                        pltpu.VMEM((hps, ATTN_CHUNKS_PER_ITER, KEY_CHUNK, nq), BF16)],
        compiler_params=pltpu.CompilerParams(
            dimension_semantics=("parallel", "parallel"), vmem_limit_bytes=VMEM_LIMIT),
        name="stickbreak_attn",
    )(qt, k5, v5)


def _top_k_gates(logits):
    n, e = logits.shape
    lane = lax.broadcasted_iota(jnp.int32, (n, e), 1)
    slot = lax.broadcasted_iota(jnp.int32, (n, TOP_K), 1)
    top_e = jnp.zeros((n, TOP_K), jnp.int32)
    top_l = jnp.zeros((n, TOP_K), F32)
    cur = logits
    for kk in range(TOP_K):
        m = jnp.max(cur, axis=-1, keepdims=True)
        idx = jnp.min(jnp.where(cur == m, lane, e), axis=-1, keepdims=True)
        top_e = jnp.where(slot == kk, idx, top_e)
        top_l = jnp.where(slot == kk, m, top_l)
        cur = jnp.where(lane == idx, -jnp.inf, cur)
    ex = jnp.exp(top_l - top_l[:, 0:1])
    return top_e, ex / jnp.sum(ex, axis=-1, keepdims=True)


def _post_kernel(at_ref, conv_ref, h_ref, ga_ref, wa_ref, wcv_ref, gf_ref, wr_ref, br_ref, *rest, meta):
    h1_ref, xn_ref, te_ref, gt_ref = rest[-4:]
    att = at_ref[0, 0].T
    convn = conv_ref[0]
    h = h_ref[...] if meta else h_ref[0]
    if meta:
        att, convn = att[ROW_BLOCK - N_META:, :], convn[ROW_BLOCK - N_META:, :]
    attn = _rms(att, ga_ref[...]).astype(BF16)
    h1 = (h + jnp.dot(attn, wa_ref[...], preferred_element_type=F32)
          + jnp.dot(convn, wcv_ref[...], preferred_element_type=F32))
    xn = _rms(h1, gf_ref[...])
    logits = jnp.dot(xn, wr_ref[...], preferred_element_type=F32, precision=lax.Precision.HIGHEST) + br_ref[...]
    top_e, gates = _top_k_gates(logits)
    h1_ref[...] = h1
    xn_ref[...] = xn.astype(BF16)
    te_ref[...] = top_e
    gt_ref[...] = gates


def _post(att_t, convn, x, meta_tokens, g_attn, w_out_a, w_out_c, g_ffn, w_router, b_router):
    b, seq, d = x.shape
    a = att_t.shape[2]
    cw = convn.shape[2]
    e = w_router.shape[1]
    nreal = b * seq
    ntok = nreal + b * N_META
    nblk = seq // ROW_BLOCK
    out_shape = [
        jax.ShapeDtypeStruct((ntok, d), F32),
        jax.ShapeDtypeStruct((ntok, d), BF16),
        jax.ShapeDtypeStruct((ntok, TOP_K), jnp.int32),
        jax.ShapeDtypeStruct((ntok, TOP_K), F32),
    ]
    weights = (g_attn, w_out_a, w_out_c, g_ffn, w_router, b_router)

    def wspecs(nargs):
        return [pl.BlockSpec(w.shape, lambda *idx, n=w.ndim: (0,) * n) for w in weights]

    def ospecs(rows, index):
        return [pl.BlockSpec((rows, d), index), pl.BlockSpec((rows, d), index),
                pl.BlockSpec((rows, TOP_K), index), pl.BlockSpec((rows, TOP_K), index)]

    real = pl.pallas_call(
        functools.partial(_post_kernel, meta=False),
        grid=(b, nblk),
        in_specs=[
            pl.BlockSpec((1, 1, a, ROW_BLOCK), lambda bi, j: (bi, j + 1, 0, 0)),
            pl.BlockSpec((1, ROW_BLOCK, cw), lambda bi, j: (bi, j + 1, 0)),
            pl.BlockSpec((1, ROW_BLOCK, d), lambda bi, j: (bi, j, 0)),
        ] + wspecs(2),
        out_specs=ospecs(ROW_BLOCK, lambda bi, j: (bi * nblk + j, 0)),
        out_shape=out_shape,
        compiler_params=pltpu.CompilerParams(
            dimension_semantics=("parallel", "parallel"), vmem_limit_bytes=VMEM_LIMIT),
        name="post_attn_real",
    )(att_t, convn, x, *weights)

    meta_blk0 = nreal // N_META
    return pl.pallas_call(
        functools.partial(_post_kernel, meta=True),
        grid=(b,),
        in_specs=[
            pl.BlockSpec((1, 1, a, ROW_BLOCK), lambda bi: (bi, 0, 0, 0)),
            pl.BlockSpec((1, ROW_BLOCK, cw), lambda bi: (bi, 0, 0)),
            pl.BlockSpec((N_META, d), lambda bi: (0, 0)),
        ] + wspecs(1) + [pl.BlockSpec(memory_space=pl.ANY)] * 4,
        out_specs=ospecs(N_META, lambda bi: (meta_blk0 + bi, 0)),
        out_shape=out_shape,
        input_output_aliases={9: 0, 10: 1, 11: 2, 12: 3},
        compiler_params=pltpu.CompilerParams(
            dimension_semantics=("parallel",), vmem_limit_bytes=VMEM_LIMIT),
        name="post_attn_meta",
    )(att_t, convn, meta_tokens, *weights, *real)


def _rank_one_hot(te, n_experts, shift):
    t = te.shape[1]
    eio = lax.broadcasted_iota(jnp.int32, (n_experts, t), 0)
    ind = te[0:1, :] == eio
    for kk in range(1, TOP_K):
        ind = ind | (te[kk:kk + 1, :] == eio)
    before = (lax.broadcasted_iota(jnp.int32, (t, t), 0) < lax.broadcasted_iota(jnp.int32, (t, t), 1)).astype(BF16)
    rank = jnp.dot(ind.astype(BF16), before, preferred_element_type=F32)
    rank = jnp.where(ind, rank - shift, -1.0)
    rio = lax.broadcasted_iota(jnp.int32, (SLAB, t), 0).astype(F32)
    sel = [jnp.broadcast_to(rank[e:e + 1, :], (SLAB, t)) == rio for e in range(n_experts)]
    return jnp.concatenate(sel, axis=0), ind


def _dispatch_kernel(base_ref, cnt_ref, npass_ref, rend_ref, te_ref, xn_ref, xs_hbm, slab, zbuf, sem, pend):
    b = pl.program_id(0)
    n_experts = rend_ref.shape[0]
    lt = xn_ref.shape[1] // 128
    srows = SLAB * lt
    slot = b % 2

    def slab_copy(sl, e, p):
        dst = pl.multiple_of((base_ref[b * n_experts + e] + p * SLAB) * lt, lt)
        return pltpu.make_async_copy(slab.at[sl, pl.ds(e * srows, srows), :], xs_hbm.at[pl.ds(dst, srows), :],
                                     sem.at[sl])

    def wait_slot(sl):
        @pl.loop(0, pend[sl])
        def _(_):
            slab_copy(sl, 0, 0).wait()
        pend[sl] = 0

    @pl.when(b == 0)
    def _():
        zbuf[...] = jnp.zeros(zbuf.shape, F32)
        zrows = zbuf.shape[0]

        def clear(e):
            dst = pl.multiple_of(jnp.maximum(rend_ref[e] * lt - zrows, 0), lt)
            return pltpu.make_async_copy(zbuf, xs_hbm.at[pl.ds(dst, zrows), :], sem.at[0])

        for e in range(n_experts):
            clear(e).start()
            clear(e).wait()
        pend[0] = 0
        pend[1] = 0

    def fill(sl, p):
        sel, _ = _rank_one_hot(te_ref[...], n_experts, (p * SLAB).astype(F32))
        rows = jnp.dot(sel.astype(BF16), xn_ref[...], preferred_element_type=F32)
        for c in range(lt):
            slab[sl, pl.ds(c, n_experts * SLAB, stride=lt), :] = rows[:, c * 128:(c + 1) * 128]

    fill(slot, jnp.int32(0))
    wait_slot(1 - slot)
    for e in range(n_experts):
        slab_copy(slot, e, 0).start()
    pend[slot] = n_experts

    @pl.loop(1, npass_ref[b])
    def _(p):
        wait_slot(slot)
        fill(slot, p)
        for e in range(n_experts):
            @pl.when(cnt_ref[b * n_experts + e] > p * SLAB)
            def _():
                cp = slab_copy(slot, e, p)
                cp.start()
                cp.wait()

    @pl.when(b == pl.num_programs(0) - 1)
    def _():
        wait_slot(slot)
        wait_slot(1 - slot)


def _dispatch(top_et, xn, base, cnt, npass, rend, cap):
    ntok, d = xn.shape
    n_experts = rend.shape[0]
    lt = d // 128
    assert d % 128 == 0 and lt % SUBLANES == 0 and ntok % TOK_BLOCK == 0
    grid_spec = pltpu.PrefetchScalarGridSpec(
        num_scalar_prefetch=4,
        grid=(ntok // TOK_BLOCK,),
        in_specs=[pl.BlockSpec((TOP_K, TOK_BLOCK), lambda b, *_: (0, b)),
                  pl.BlockSpec((TOK_BLOCK, d), lambda b, *_: (b, 0))],
        out_specs=pl.BlockSpec(memory_space=pl.ANY),
        scratch_shapes=[pltpu.VMEM((2, n_experts * SLAB * lt, 128), F32),
                        pltpu.VMEM((2 * MOE_ROWS * lt, 128), F32),
                        pltpu.SemaphoreType.DMA((2,)),
                        pltpu.SMEM((2,), jnp.int32)],
    )
    return pl.pallas_call(
        _dispatch_kernel,
        grid_spec=grid_spec,
        out_shape=jax.ShapeDtypeStruct((cap * lt, 128), F32),
        compiler_params=pltpu.CompilerParams(dimension_semantics=("arbitrary",), vmem_limit_bytes=VMEM_LIMIT),
        name="moe_dispatch",
    )(base, cnt, npass, rend, top_et, xn)


def _ffn_kernel(bexp_ref, nused_ref, xs_ref, wgu_ref, bgu_ref, wd_ref, bd_ref, ys_ref, wgu_bf, wd_bf):
    i = pl.program_id(0)
    ff, d = wd_bf.shape
    lt = d // 128

    @pl.when(i < nused_ref[0])
    def _():
        @pl.when((i == 0) | (bexp_ref[i] != bexp_ref[jnp.maximum(i - 1, 0)]))
        def _():
            wgu_bf[...] = wgu_ref[0].astype(BF16)
            wd_bf[...] = wd_ref[0].astype(BF16)

        x = jnp.concatenate([xs_ref[pl.ds(c, MOE_ROWS, stride=lt), :] for c in range(lt)], axis=1).astype(BF16)
        gu = jnp.dot(x, wgu_bf[...], preferred_element_type=F32) + bgu_ref[0]
        gate = jnp.minimum(gu[:, :ff], SWIGLU_LIMIT)
        lin = jnp.clip(gu[:, ff:], -SWIGLU_LIMIT, SWIGLU_LIMIT)
        hid = (lin + 1.0) * (gate * jax.nn.sigmoid(SWIGLU_ALPHA * gate))
        y = jnp.dot(hid.astype(BF16), wd_bf[...], preferred_element_type=F32) + bd_ref[0]
        for c in range(lt):
            ys_ref[pl.ds(c, MOE_ROWS, stride=lt), :] = y[:, c * 128:(c + 1) * 128]

    @pl.when(i >= nused_ref[0])
    def _():
        ys_ref[...] = jnp.zeros(ys_ref.shape, F32)


def _ffn(xs, block_expert, n_used, w_gu, b_gu, w_down, b_down):
    e, d, ff2 = w_gu.shape
    ff = ff2 // 2
    lt = d // 128
    nblocks = block_expert.shape[0]
    rows = MOE_ROWS * lt
    grid_spec = pltpu.PrefetchScalarGridSpec(
        num_scalar_prefetch=2,
        grid=(nblocks,),
        in_specs=[
            pl.BlockSpec((rows, 128), lambda i, be, nu: (i, 0)),
            pl.BlockSpec((1, d, ff2), lambda i, be, nu: (be[i], 0, 0)),
            pl.BlockSpec((1, 1, ff2), lambda i, be, nu: (be[i], 0, 0)),
            pl.BlockSpec((1, ff, d), lambda i, be, nu: (be[i], 0, 0)),
            pl.BlockSpec((1, 1, d), lambda i, be, nu: (be[i], 0, 0)),
        ],
        out_specs=pl.BlockSpec((rows, 128), lambda i, be, nu: (i, 0)),
        scratch_shapes=[pltpu.VMEM((d, ff2), BF16), pltpu.VMEM((ff, d), BF16)],
    )
    return pl.pallas_call(
        _ffn_kernel,
        grid_spec=grid_spec,
        out_shape=jax.ShapeDtypeStruct(xs.shape, F32),
        compiler_params=pltpu.CompilerParams(dimension_semantics=("arbitrary",), vmem_limit_bytes=VMEM_LIMIT),
        name="moe_experts",
    )(block_expert, n_used, xs, w_gu, b_gu.reshape(e, 1, ff2), w_down, b_down.reshape(e, 1, d))


def _route_tables(top_e, n_experts):
    ntok = top_e.shape[0]
    nb = ntok // TOK_BLOCK
    ids = jnp.arange(n_experts, dtype=jnp.int32)
    cnt = jnp.sum(top_e.reshape(nb, TOK_BLOCK * TOP_K, 1) == ids, axis=1, dtype=jnp.int32)
    total = jnp.sum(cnt, axis=0)
    region = (total + SLAB + MOE_ROWS - 1) // MOE_ROWS * MOE_ROWS
    rend = jnp.cumsum(region)
    base = (rend - region)[None, :] + jnp.cumsum(cnt, axis=0) - cnt
    npass = jnp.maximum((jnp.max(cnt, axis=1) + SLAB - 1) // SLAB, 1)
    nblocks = -(-(ntok * TOP_K + n_experts * (SLAB + MOE_ROWS - 1)) // MOE_ROWS)
    block_start = jnp.arange(nblocks, dtype=jnp.int32) * MOE_ROWS
    block_expert = jnp.minimum(jnp.sum(rend[None, :] <= block_start[:, None], axis=1), n_experts - 1)
    n_used = (rend[-1] // MOE_ROWS).reshape(1)
    i32 = lambda v: v.astype(jnp.int32)
    return (i32(base.reshape(-1)), i32(cnt.reshape(-1)), i32(npass), i32(rend), i32(block_expert), i32(n_used),
            nblocks * MOE_ROWS)


def _combine_kernel(base_ref, cnt_ref, npass_ref, te_ref, gt_ref, h1_ref, g_ref, ys_hbm, o_ref, ybuf, acc, sem):
    b = pl.program_id(0)
    nb = pl.num_programs(0)
    n_experts = ybuf.shape[1] // (SLAB * (h1_ref.shape[1] // 128))
    lt = h1_ref.shape[1] // 128
    srows = SLAB * lt
    slot = b % 2

    def slab_copy(blk, sl, e, p):
        src = pl.multiple_of((base_ref[blk * n_experts + e] + p * SLAB) * lt, lt)
        return pltpu.make_async_copy(ys_hbm.at[pl.ds(src, srows), :], ybuf.at[sl, pl.ds(e * srows, srows), :],
                                     sem.at[sl])

    def fetch(blk, sl):
        for e in range(n_experts):
            slab_copy(blk, sl, e, 0).start()

    @pl.when(b == 0)
    def _():
        fetch(b, slot)

    @pl.when(b + 1 < nb)
    def _():
        fetch(b + 1, 1 - slot)

    for e in range(n_experts):
        slab_copy(b, slot, e, 0).wait()

    def gathered(p):
        te = te_ref[...]
        sel, ind = _rank_one_hot(te, n_experts, (p * SLAB).astype(F32))
        eio = lax.broadcasted_iota(jnp.int32, ind.shape, 0)
        gate_te = jnp.zeros(ind.shape, F32)
        for kk in range(TOP_K):
            gate_te = gate_te + jnp.where(te[kk:kk + 1, :] == eio, gt_ref[kk:kk + 1, :], 0.0)
        gate_rows = jnp.concatenate(
            [jnp.sum(jnp.where(sel[e * SLAB:(e + 1) * SLAB], gate_te[e:e + 1, :], 0.0), axis=1, keepdims=True)
             for e in range(n_experts)], axis=0)
        y = jnp.concatenate([ybuf[slot, pl.ds(c, n_experts * SLAB, stride=lt), :] for c in range(lt)], axis=1)
        y = y * gate_rows
        hi = y.astype(BF16)
        lo = (y - hi.astype(F32)).astype(BF16)
        tn_dims = (((0,), (0,)), ((), ()))
        onehot = sel.astype(BF16)
        return (lax.dot_general(onehot, hi, tn_dims, preferred_element_type=F32)
                + lax.dot_general(onehot, lo, tn_dims, preferred_element_type=F32))

    acc[...] = h1_ref[...] + gathered(jnp.int32(0))

    @pl.loop(1, npass_ref[b])
    def _(p):
        for e in range(n_experts):
            @pl.when(cnt_ref[b * n_experts + e] > p * SLAB)
            def _():
                cp = slab_copy(b, slot, e, p)
                cp.start()
                cp.wait()
        acc[...] += gathered(p)

    o_ref[...] = _rms(acc[...], g_ref[...])


def _combine(h1, top_et, gates_t, ys, base, cnt, npass, g_final, nreal, n_experts):
    d = h1.shape[1]
    lt = d // 128
    grid_spec = pltpu.PrefetchScalarGridSpec(
        num_scalar_prefetch=3,
        grid=(nreal // TOK_BLOCK,),
        in_specs=[pl.BlockSpec((TOP_K, TOK_BLOCK), lambda b, *_: (0, b)),
                  pl.BlockSpec((TOP_K, TOK_BLOCK), lambda b, *_: (0, b)),
                  pl.BlockSpec((TOK_BLOCK, d), lambda b, *_: (b, 0)),
                  pl.BlockSpec((1, d), lambda b, *_: (0, 0)),
                  pl.BlockSpec(memory_space=pl.ANY)],
        out_specs=pl.BlockSpec((TOK_BLOCK, d), lambda b, *_: (b, 0)),
        scratch_shapes=[pltpu.VMEM((2, n_experts * SLAB * lt, 128), F32),
                        pltpu.VMEM((TOK_BLOCK, d), F32),
                        pltpu.SemaphoreType.DMA((2,))],
    )
    return pl.pallas_call(
        _combine_kernel,
        grid_spec=grid_spec,
        out_shape=jax.ShapeDtypeStruct((nreal, d), F32),
        compiler_params=pltpu.CompilerParams(dimension_semantics=("arbitrary",), vmem_limit_bytes=VMEM_LIMIT),
        name="moe_combine",
    )(base, cnt, npass, top_et, gates_t, h1, g_final, ys)


def kernel(x, meta_tokens, g_mix, w_in, g_attn_out, conv_w, conv_b, g_conv_out, w_out, g_ffn,
           w_router, b_router, w_gu, b_gu, w_down, b_down, g_final):
    b, seq, d = x.shape
    depth = w_in.shape[0]
    assert depth == 1 and seq % ROW_BLOCK == 0
    a = w_out.shape[1] // 2
    n_experts = w_router.shape[2]

    meta_blk = jnp.concatenate([jnp.zeros((ROW_BLOCK - N_META, d), x.dtype), meta_tokens], axis=0)
    w = w_in[0]
    scale = HEAD_DIM ** -0.5 * LOG2_E
    wqt = (w[:, :a] * scale).T.astype(BF16)
    wk = w[:, a:2 * a].astype(BF16)
    wvt = w[:, 2 * a:3 * a].T.astype(BF16)
    wc = w[:, 3 * a:].astype(BF16)
    qt, k5, v5, convn = _in_proj(x, meta_blk, g_mix, wqt, wk, wvt, wc, conv_w[0], conv_b, g_conv_out)
    att_t = _attention(qt, k5, v5)

    wo = w_out[0].astype(BF16)
    h1, xn, top_e, gates = _post(att_t, convn, x, meta_tokens, g_attn_out, wo[:a], wo[a:], g_ffn,
                                 w_router[0], b_router)

    base, cnt, npass, rend, block_expert, n_used, cap = _route_tables(top_e, n_experts)
    top_et, gates_t = top_e.T, gates.T
    xs = _dispatch(top_et, xn, base, cnt, npass, rend, cap)
    ys = _ffn(xs, block_expert, n_used, w_gu[0], b_gu[0], w_down[0], b_down[0])
    out = _combine(h1, top_et, gates_t, ys, base, cnt, npass, g_final.reshape(1, d), b * seq, n_experts)
    return out.reshape(b, seq, d)
```

```python
import functools

import jax
import jax.numpy as jnp
from jax import lax
from jax.experimental import pallas as pl
from jax.experimental.pallas import tpu as pltpu

N_META = 16
HEAD_DIM = 64
CONV_K = 3
TOP_K = 4
SWIGLU_LIMIT = 7.0
SWIGLU_ALPHA = 1.702
RMS_EPS = 1e-6

SUBLANES = 8
ROW_BLOCK = 256
KEY_CHUNK = 128
KEY_GROUPS = KEY_CHUNK // SUBLANES
ATTN_CHUNKS_PER_ITER = ROW_BLOCK // KEY_CHUNK
ATTN_HEADS_PER_STEP = 2
STICK_DEAD_BITS = 160.0
LOG2_E = 1.4426950408889634
MOE_ROWS = 256
TOK_BLOCK = 128
SLAB = 32
SLAB_SHORT = 24
CARRY_ROWS = SUBLANES
VMEM_LIMIT = 48 * 1024 * 1024

F32 = jnp.float32
BF16 = jnp.bfloat16


def _rms(x, g):
    return x * lax.rsqrt(jnp.mean(x * x, axis=-1, keepdims=True) + RMS_EPS) * g


def _in_proj_kernel(x_ref, meta_ref, g_ref, perm_ref, wqt_ref, wk_ref, wvt_ref, wc_ref, cw_ref, cb_ref, gc_ref,
                    qt_ref, k_ref, vt_ref, conv_ref, u_scr):
    j = pl.program_id(1)
    rows = x_ref.shape[1]
    cw = conv_ref.shape[2]
    h = jnp.where(j == 0, meta_ref[...], x_ref[0])
    nb = _rms(h, g_ref[...]).astype(BF16)
    nt_dims = (((1,), (1,)), ((), ()))
    qt_ref[0, 0] = lax.dot_general(wqt_ref[...], nb, nt_dims, preferred_element_type=F32).astype(BF16)
    nperm = jnp.dot(perm_ref[...], nb, preferred_element_type=F32).astype(BF16)
    kp = jnp.dot(nperm, wk_ref[...], preferred_element_type=F32).astype(BF16)
    vtp = lax.dot_general(wvt_ref[...], nperm, nt_dims, preferred_element_type=F32).astype(BF16)
    for hh in range(k_ref.shape[1]):
        for c in range(rows // KEY_CHUNK):
            k_ref[0, hh, c] = kp[c * KEY_CHUNK:(c + 1) * KEY_CHUNK, hh * HEAD_DIM:(hh + 1) * HEAD_DIM]
            vt_ref[0, hh, c] = vtp[hh * HEAD_DIM:(hh + 1) * HEAD_DIM, c * KEY_CHUNK:(c + 1) * KEY_CHUNK]
    cp = jnp.dot(nb, wc_ref[...], preferred_element_type=F32)
    gate_b, u = cp[:, :cw], cp[:, cw:2 * cw] * cp[:, 2 * cw:]

    @pl.when(j == 0)
    def _():
        u_scr[0:CARRY_ROWS, :] = jnp.zeros((CARRY_ROWS, cw), F32)

    u_scr[CARRY_ROWS:CARRY_ROWS + rows, :] = u
    u1 = u_scr[CARRY_ROWS - 1:CARRY_ROWS - 1 + rows, :]
    u2 = u_scr[CARRY_ROWS - 2:CARRY_ROWS - 2 + rows, :]
    y = cw_ref[0:1, :] * u2 + cw_ref[1:2, :] * u1 + cw_ref[2:3, :] * u
    u_scr[0:CARRY_ROWS, :] = u_scr[rows:rows + CARRY_ROWS, :]
    conv = gate_b * (y + cb_ref[...])
    conv_ref[0] = _rms(conv, gc_ref[...]).astype(BF16)


def _key_order_permutation():
    p = jnp.arange(ROW_BLOCK)
    chunk, g, r = p // KEY_CHUNK, (p % KEY_CHUNK) // SUBLANES, p % SUBLANES
    src = chunk * KEY_CHUNK + r * KEY_GROUPS + g
    return (src[:, None] == jnp.arange(ROW_BLOCK)[None, :]).astype(BF16)


def _in_proj(x, meta_blk, g_mix, wqt, wk, wvt, wc, conv_w, conv_b, g_conv):
    b, seq, d = x.shape
    a = wk.shape[1]
    cw = wc.shape[1] // 3
    heads = a // HEAD_DIM
    nblk = seq // ROW_BLOCK + 1
    tp = nblk * ROW_BLOCK
    cpb = ROW_BLOCK // KEY_CHUNK
    full = lambda shp: pl.BlockSpec(shp, lambda bi, j: (0,) * len(shp))
    return pl.pallas_call(
        _in_proj_kernel,
        grid=(b, nblk),
        in_specs=[
            pl.BlockSpec((1, ROW_BLOCK, d), lambda bi, j: (bi, jnp.maximum(j - 1, 0), 0)),
            full((ROW_BLOCK, d)), full((1, d)), full((ROW_BLOCK, ROW_BLOCK)),
            full((a, d)), full((d, a)), full((a, d)), full((d, 3 * cw)),
            full((CONV_K, cw)), full((1, cw)), full((1, cw)),
        ],
        out_specs=[
            pl.BlockSpec((1, 1, a, ROW_BLOCK), lambda bi, j: (bi, j, 0, 0)),
            pl.BlockSpec((1, heads, cpb, KEY_CHUNK, HEAD_DIM), lambda bi, j: (bi, 0, j, 0, 0)),
            pl.BlockSpec((1, heads, cpb, HEAD_DIM, KEY_CHUNK), lambda bi, j: (bi, 0, j, 0, 0)),
            pl.BlockSpec((1, ROW_BLOCK, cw), lambda bi, j: (bi, j, 0)),
        ],
        out_shape=[
            jax.ShapeDtypeStruct((b, nblk, a, ROW_BLOCK), BF16),
            jax.ShapeDtypeStruct((b, heads, tp // KEY_CHUNK, KEY_CHUNK, HEAD_DIM), BF16),
            jax.ShapeDtypeStruct((b, heads, tp // KEY_CHUNK, HEAD_DIM, KEY_CHUNK), BF16),
            jax.ShapeDtypeStruct((b, tp, cw), BF16),
        ],
        scratch_shapes=[pltpu.VMEM((CARRY_ROWS + ROW_BLOCK, cw), F32)],
        compiler_params=pltpu.CompilerParams(
            dimension_semantics=("parallel", "arbitrary"), vmem_limit_bytes=VMEM_LIMIT),
        name="in_proj",
    )(x, meta_blk, g_mix, _key_order_permutation(), wqt, wk, wvt, wc, conv_w, conv_b, g_conv)


def _attn_kernel(qt_ref, k_ref, vt_ref, o_ref, z_scr, zd_scr, w_scr):
    nblk, nq = qt_ref.shape[1], qt_ref.shape[3]
    hd = HEAD_DIM
    heads = range(ATTN_HEADS_PER_STEP)
    per_iter = ATTN_CHUNKS_PER_ITER
    shape3 = (KEY_GROUPS, SUBLANES, nq)
    key_in_chunk = (lax.broadcasted_iota(jnp.int32, shape3, 1) * KEY_GROUPS
                    + lax.broadcasted_iota(jnp.int32, shape3, 0))
    lane = lax.broadcasted_iota(jnp.int32, shape3, 2)
    sub = lax.broadcasted_iota(jnp.int32, (SUBLANES, nq), 0)
    step_top = lambda i, t: (i + 1 - t) * per_iter - 1

    def scores(dst, i, jtop):
        for h in heads:
            qt = qt_ref[0, i, h * hd:(h + 1) * hd, :]
            for c in range(per_iter):
                dst[h, c] = jnp.dot(k_ref[0, h, jtop - c], qt, preferred_element_type=F32)

    def weights(src, carries, i, jtop, masked):
        return tuple(head_weights(src, h, carries[h], i, jtop, masked) for h in heads)

    def head_weights(src, h, carry, i, jtop, masked):
        for c in range(per_iter):
            z = src[h, c].reshape(shape3)
            sp = jnp.maximum(z, jnp.log(1.0 + jnp.exp2(jnp.minimum(z, 64.0))) * LOG2_E)
            if masked:
                valid = (jtop - c) * KEY_CHUNK + key_in_chunk < i * nq + lane
                sp = jnp.where(valid, sp, 0.0)
            runs = [None] * KEY_GROUPS
            run = sp[KEY_GROUPS - 1]
            runs[KEY_GROUPS - 1] = run
            for g in range(KEY_GROUPS - 2, -1, -1):
                run = run + sp[g]
                runs[g] = run
            tail = run
            for shift in (1, 2, 4):
                tail = tail + jnp.where(sub < SUBLANES - shift, pltpu.roll(tail, SUBLANES - shift, 0), 0.0)
            off = carry + (tail - run)
            w = jnp.exp2(z - (jnp.stack(runs, axis=0) + off[None]))
            if masked:
                w = jnp.where(valid, w, 0.0)
            w_scr[h, c] = w.reshape(KEY_CHUNK, nq).astype(BF16)
            carry = carry + tail[0:1, :]
        return carry

    def weighted_values(accs, jtop):
        accs = list(accs)
        for h in heads:
            for c in range(per_iter):
                accs[h] = accs[h] + jnp.dot(vt_ref[0, h, jtop - c], w_scr[h, c], preferred_element_type=F32)
        return tuple(accs)

    def store(i, accs):
        for h in heads:
            o_ref[0, i, h * hd:(h + 1) * hd, :] = accs[h]

    zero_carry = tuple(jnp.zeros((1, nq), F32) for _ in heads)
    zero_acc = tuple(jnp.zeros((hd, nq), F32) for _ in heads)
    w_scr[...] = jnp.zeros(w_scr.shape, BF16)
    scores(zd_scr, 0, step_top(0, 0))

    def stick_left(carries):
        m = carries[0]
        for h in heads[1:]:
            m = jnp.minimum(m, carries[h])
        return jnp.min(m) < STICK_DEAD_BITS

    def query_block(i, state):
        accs, pending_top = state
        store(jnp.maximum(i - 1, 0), weighted_values(accs, pending_top))
        carries = weights(zd_scr, zero_carry, i, step_top(i, 0), True)
        nxt = jnp.minimum(i + 1, nblk - 1)
        scores(zd_scr, nxt, step_top(nxt, 0))
        scores(z_scr, i, jnp.maximum(step_top(i, 1), per_iter - 1))

        def step(state):
            t, _, carries, accs = state
            live = stick_left(carries)
            accs = weighted_values(accs, step_top(i, t - 1))
            carries = weights(z_scr, carries, i, step_top(i, t), False)
            scores(z_scr, i, jnp.maximum(step_top(i, t + 1), per_iter - 1))
            return t + 1, live, carries, accs

        t_end, _, _, accs = lax.while_loop(lambda s: (s[0] <= i) & s[1], step,
                                           (jnp.int32(1), jnp.bool_(True), carries, zero_acc))
        return accs, step_top(i, t_end - 1)

    accs, pending_top = lax.fori_loop(0, nblk, query_block, (zero_acc, jnp.int32(per_iter - 1)))
    store(nblk - 1, weighted_values(accs, pending_top))


def _attention(qt, k5, v5):
    b, nblk, a, nq = qt.shape
    heads = a // HEAD_DIM
    nchunks = k5.shape[2]
    hps = ATTN_HEADS_PER_STEP
    return pl.pallas_call(
        _attn_kernel,
        grid=(b, heads // hps),
        in_specs=[
            pl.BlockSpec((1, nblk, hps * HEAD_DIM, nq), lambda bi, h: (bi, 0, h, 0)),
            pl.BlockSpec((1, hps, nchunks, KEY_CHUNK, HEAD_DIM), lambda bi, h: (bi, h, 0, 0, 0)),
            pl.BlockSpec((1, hps, nchunks, HEAD_DIM, KEY_CHUNK), lambda bi, h: (bi, h, 0, 0, 0)),
        ],
        out_specs=pl.BlockSpec((1, nblk, hps * HEAD_DIM, nq), lambda bi, h: (bi, 0, h, 0)),
        out_shape=jax.ShapeDtypeStruct((b, nblk, a, nq), F32),
        scratch_shapes=[pltpu.VMEM((hps, ATTN_CHUNKS_PER_ITER, KEY_CHUNK, nq), F32),
                        pltpu.VMEM((hps, ATTN_CHUNKS_PER_ITER, KEY_CHUNK, nq), F32),
                        pltpu.VMEM((hps, ATTN_CHUNKS_PER_ITER, KEY_CHUNK, nq), BF16)],
        compiler_params=pltpu.CompilerParams(
            dimension_semantics=("parallel", "parallel"), vmem_limit_bytes=VMEM_LIMIT),
        name="stickbreak_attn",
    )(qt, k5, v5)


def _top_k_gates(logits):
    n, e = logits.shape
    lane = lax.broadcasted_iota(jnp.int32, (n, e), 1)
    slot = lax.broadcasted_iota(jnp.int32, (n, TOP_K), 1)
    top_e = jnp.zeros((n, TOP_K), jnp.int32)
    top_l = jnp.zeros((n, TOP_K), F32)
    cur = logits
    for kk in range(TOP_K):
        m = jnp.max(cur, axis=-1, keepdims=True)
        idx = jnp.min(jnp.where(cur == m, lane, e), axis=-1, keepdims=True)
        top_e = jnp.where(slot == kk, idx, top_e)
        top_l = jnp.where(slot == kk, m, top_l)
        cur = jnp.where(lane == idx, -jnp.inf, cur)
    ex = jnp.exp(top_l - top_l[:, 0:1])
    return top_e, ex / jnp.sum(ex, axis=-1, keepdims=True)


def _post_kernel(at_ref, conv_ref, h_ref, ga_ref, wa_ref, wcv_ref, gf_ref, wr_ref, br_ref, *rest, meta):
    h1_ref, xn_ref, te_ref, gt_ref = rest[-4:]
    att = at_ref[0, 0].T
    convn = conv_ref[0]
    h = h_ref[...] if meta else h_ref[0]
    if meta:
        att, convn = att[ROW_BLOCK - N_META:, :], convn[ROW_BLOCK - N_META:, :]
    attn = _rms(att, ga_ref[...]).astype(BF16)
    h1 = (h + jnp.dot(attn, wa_ref[...], preferred_element_type=F32)
          + jnp.dot(convn, wcv_ref[...], preferred_element_type=F32))
    xn = _rms(h1, gf_ref[...])
    logits = jnp.dot(xn, wr_ref[...], preferred_element_type=F32, precision=lax.Precision.HIGHEST) + br_ref[...]
    top_e, gates = _top_k_gates(logits)
    h1_ref[...] = h1
    xn_ref[...] = xn.astype(BF16)
    te_ref[...] = top_e
    gt_ref[...] = gates


def _post(att_t, convn, x, meta_tokens, g_attn, w_out_a, w_out_c, g_ffn, w_router, b_router):
    b, seq, d = x.shape
    a = att_t.shape[2]
    cw = convn.shape[2]
    e = w_router.shape[1]
    nreal = b * seq
    ntok = nreal + b * N_META
    nblk = seq // ROW_BLOCK
    out_shape = [
        jax.ShapeDtypeStruct((ntok, d), F32),
        jax.ShapeDtypeStruct((ntok, d), BF16),
        jax.ShapeDtypeStruct((ntok, TOP_K), jnp.int32),
        jax.ShapeDtypeStruct((ntok, TOP_K), F32),
    ]
    weights = (g_attn, w_out_a, w_out_c, g_ffn, w_router, b_router)

    def wspecs(nargs):
        return [pl.BlockSpec(w.shape, lambda *idx, n=w.ndim: (0,) * n) for w in weights]

    def ospecs(rows, index):
        return [pl.BlockSpec((rows, d), index), pl.BlockSpec((rows, d), index),
                pl.BlockSpec((rows, TOP_K), index), pl.BlockSpec((rows, TOP_K), index)]

    real = pl.pallas_call(
        functools.partial(_post_kernel, meta=False),
        grid=(b, nblk),
        in_specs=[
            pl.BlockSpec((1, 1, a, ROW_BLOCK), lambda bi, j: (bi, j + 1, 0, 0)),
            pl.BlockSpec((1, ROW_BLOCK, cw), lambda bi, j: (bi, j + 1, 0)),
            pl.BlockSpec((1, ROW_BLOCK, d), lambda bi, j: (bi, j, 0)),
        ] + wspecs(2),
        out_specs=ospecs(ROW_BLOCK, lambda bi, j: (bi * nblk + j, 0)),
        out_shape=out_shape,
        compiler_params=pltpu.CompilerParams(
            dimension_semantics=("parallel", "parallel"), vmem_limit_bytes=VMEM_LIMIT),
        name="post_attn_real",
    )(att_t, convn, x, *weights)

    meta_blk0 = nreal // N_META
    return pl.pallas_call(
        functools.partial(_post_kernel, meta=True),
        grid=(b,),
        in_specs=[
            pl.BlockSpec((1, 1, a, ROW_BLOCK), lambda bi: (bi, 0, 0, 0)),
            pl.BlockSpec((1, ROW_BLOCK, cw), lambda bi: (bi, 0, 0)),
            pl.BlockSpec((N_META, d), lambda bi: (0, 0)),
        ] + wspecs(1) + [pl.BlockSpec(memory_space=pl.ANY)] * 4,
        out_specs=ospecs(N_META, lambda bi: (meta_blk0 + bi, 0)),
        out_shape=out_shape,
        input_output_aliases={9: 0, 10: 1, 11: 2, 12: 3},
        compiler_params=pltpu.CompilerParams(
            dimension_semantics=("parallel",), vmem_limit_bytes=VMEM_LIMIT),
        name="post_attn_meta",
    )(att_t, convn, meta_tokens, *weights, *real)


def _rank_one_hot(te, n_experts, shift):
    t = te.shape[1]
    eio = lax.broadcasted_iota(jnp.int32, (n_experts, t), 0)
    ind = te[0:1, :] == eio
    for kk in range(1, TOP_K):
        ind = ind | (te[kk:kk + 1, :] == eio)
    before = (lax.broadcasted_iota(jnp.int32, (t, t), 0) < lax.broadcasted_iota(jnp.int32, (t, t), 1)).astype(BF16)
    rank = jnp.dot(ind.astype(BF16), before, preferred_element_type=F32)
    rank = jnp.where(ind, rank - shift, -1.0)
    rio = lax.broadcasted_iota(jnp.int32, (SLAB, t), 0).astype(F32)
    sel = [jnp.broadcast_to(rank[e:e + 1, :], (SLAB, t)) == rio for e in range(n_experts)]
    return jnp.concatenate(sel, axis=0), ind


def _dispatch_kernel(base_ref, cnt_ref, npass_ref, rend_ref, te_ref, xn_ref, xs_hbm, slab, zbuf, sem, pend):
    b = pl.program_id(0)
    n_experts = rend_ref.shape[0]
    lt = xn_ref.shape[1] // 128
    srows = SLAB * lt
    slot = b % 2

    def slab_copy(sl, e, p, full):
        nrows = srows if full else SLAB_SHORT * lt
        dst = pl.multiple_of((base_ref[b * n_experts + e] + p * SLAB) * lt, lt)
        return pltpu.make_async_copy(slab.at[sl, pl.ds(e * srows, nrows), :], xs_hbm.at[pl.ds(dst, nrows), :],
                                     sem.at[sl, int(full)])

    def wait_slot(sl):
        for full in (False, True):
            @pl.loop(0, pend[sl, int(full)])
            def _(_):
                slab_copy(sl, 0, 0, full).wait()
            pend[sl, int(full)] = 0

    @pl.when(b == 0)
    def _():
        zbuf[...] = jnp.zeros(zbuf.shape, F32)
        zrows = zbuf.shape[0]

        def clear(e):
            dst = pl.multiple_of(jnp.maximum(rend_ref[e] * lt - zrows, 0), lt)
            return pltpu.make_async_copy(zbuf, xs_hbm.at[pl.ds(dst, zrows), :], sem.at[0, 0])

        for e in range(n_experts):
            clear(e).start()
            clear(e).wait()
        for sl in range(2):
            pend[sl, 0] = 0
            pend[sl, 1] = 0

    def fill(sl, p):
        sel, _ = _rank_one_hot(te_ref[...], n_experts, (p * SLAB).astype(F32))
        rows = jnp.dot(sel.astype(BF16), xn_ref[...], preferred_element_type=F32)
        for c in range(lt):
            slab[sl, pl.ds(c, n_experts * SLAB, stride=lt), :] = rows[:, c * 128:(c + 1) * 128]

    fill(slot, jnp.int32(0))
    wait_slot(1 - slot)
    n_full = jnp.int32(0)
    for e in range(n_experts):
        full = cnt_ref[b * n_experts + e] > SLAB_SHORT
        n_full = n_full + full.astype(jnp.int32)

        @pl.when(full)
        def _():
            slab_copy(slot, e, 0, True).start()

        @pl.when(jnp.logical_not(full))
        def _():
            slab_copy(slot, e, 0, False).start()
    pend[slot, 1] = n_full
    pend[slot, 0] = n_experts - n_full

    @pl.loop(1, npass_ref[b])
    def _(p):
        wait_slot(slot)
        fill(slot, p)
        for e in range(n_experts):
            @pl.when(cnt_ref[b * n_experts + e] > p * SLAB)
            def _():
                cp = slab_copy(slot, e, p, True)
                cp.start()
                cp.wait()

    @pl.when(b == pl.num_programs(0) - 1)
    def _():
        wait_slot(slot)
        wait_slot(1 - slot)


def _dispatch(top_et, xn, base, cnt, npass, rend, cap):
    ntok, d = xn.shape
    n_experts = rend.shape[0]
    lt = d // 128
    assert d % 128 == 0 and lt % SUBLANES == 0 and ntok % TOK_BLOCK == 0
    grid_spec = pltpu.PrefetchScalarGridSpec(
        num_scalar_prefetch=4,
        grid=(ntok // TOK_BLOCK,),
        in_specs=[pl.BlockSpec((TOP_K, TOK_BLOCK), lambda b, *_: (0, b)),
                  pl.BlockSpec((TOK_BLOCK, d), lambda b, *_: (b, 0))],
        out_specs=pl.BlockSpec(memory_space=pl.ANY),
        scratch_shapes=[pltpu.VMEM((2, n_experts * SLAB * lt, 128), F32),
                        pltpu.VMEM((2 * MOE_ROWS * lt, 128), F32),
                        pltpu.SemaphoreType.DMA((2, 2)),
                        pltpu.SMEM((2, 2), jnp.int32)],
    )
    return pl.pallas_call(
        _dispatch_kernel,
        grid_spec=grid_spec,
        out_shape=jax.ShapeDtypeStruct((cap * lt, 128), F32),
        compiler_params=pltpu.CompilerParams(dimension_semantics=("arbitrary",), vmem_limit_bytes=VMEM_LIMIT),
        name="moe_dispatch",
    )(base, cnt, npass, rend, top_et, xn)


def _ffn_kernel(bexp_ref, nused_ref, xs_ref, wgu_ref, bgu_ref, wd_ref, bd_ref, ys_ref, wgu_bf, wd_bf):
    i = pl.program_id(0)
    ff, d = wd_bf.shape
    lt = d // 128

    @pl.when(i < nused_ref[0])
    def _():
        @pl.when((i == 0) | (bexp_ref[i] != bexp_ref[jnp.maximum(i - 1, 0)]))
        def _():
            wgu_bf[...] = wgu_ref[0].astype(BF16)
            wd_bf[...] = wd_ref[0].astype(BF16)

        x = jnp.concatenate([xs_ref[pl.ds(c, MOE_ROWS, stride=lt), :] for c in range(lt)], axis=1).astype(BF16)
        gu = jnp.dot(x, wgu_bf[...], preferred_element_type=F32) + bgu_ref[0]
        gate = jnp.minimum(gu[:, :ff], SWIGLU_LIMIT)
        lin = jnp.clip(gu[:, ff:], -SWIGLU_LIMIT, SWIGLU_LIMIT)
        hid = (lin + 1.0) * (gate * jax.nn.sigmoid(SWIGLU_ALPHA * gate))
        y = jnp.dot(hid.astype(BF16), wd_bf[...], preferred_element_type=F32) + bd_ref[0]
        for c in range(lt):
            ys_ref[pl.ds(c, MOE_ROWS, stride=lt), :] = y[:, c * 128:(c + 1) * 128]

    @pl.when(i >= nused_ref[0])
    def _():
        ys_ref[...] = jnp.zeros(ys_ref.shape, F32)


def _ffn(xs, block_expert, n_used, w_gu, b_gu, w_down, b_down):
    e, d, ff2 = w_gu.shape
    ff = ff2 // 2
    lt = d // 128
    nblocks = block_expert.shape[0]
    rows = MOE_ROWS * lt
    grid_spec = pltpu.PrefetchScalarGridSpec(
        num_scalar_prefetch=2,
        grid=(nblocks,),
        in_specs=[
            pl.BlockSpec((rows, 128), lambda i, be, nu: (i, 0)),
            pl.BlockSpec((1, d, ff2), lambda i, be, nu: (be[i], 0, 0)),
            pl.BlockSpec((1, 1, ff2), lambda i, be, nu: (be[i], 0, 0)),
            pl.BlockSpec((1, ff, d), lambda i, be, nu: (be[i], 0, 0)),
            pl.BlockSpec((1, 1, d), lambda i, be, nu: (be[i], 0, 0)),
        ],
        out_specs=pl.BlockSpec((rows, 128), lambda i, be, nu: (i, 0)),
        scratch_shapes=[pltpu.VMEM((d, ff2), BF16), pltpu.VMEM((ff, d), BF16)],
    )
    return pl.pallas_call(
        _ffn_kernel,
        grid_spec=grid_spec,
        out_shape=jax.ShapeDtypeStruct(xs.shape, F32),
        compiler_params=pltpu.CompilerParams(dimension_semantics=("arbitrary",), vmem_limit_bytes=VMEM_LIMIT),
        name="moe_experts",
    )(block_expert, n_used, xs, w_gu, b_gu.reshape(e, 1, ff2), w_down, b_down.reshape(e, 1, d))


def _route_tables(top_e, n_experts):
    ntok = top_e.shape[0]
    nb = ntok // TOK_BLOCK
    ids = jnp.arange(n_experts, dtype=jnp.int32)
    cnt = jnp.sum(top_e.reshape(nb, TOK_BLOCK * TOP_K, 1) == ids, axis=1, dtype=jnp.int32)
    total = jnp.sum(cnt, axis=0)
    region = (total + SLAB + MOE_ROWS - 1) // MOE_ROWS * MOE_ROWS
    rend = jnp.cumsum(region)
    base = (rend - region)[None, :] + jnp.cumsum(cnt, axis=0) - cnt
    npass = jnp.maximum((jnp.max(cnt, axis=1) + SLAB - 1) // SLAB, 1)
    nblocks = -(-(ntok * TOP_K + n_experts * (SLAB + MOE_ROWS - 1)) // MOE_ROWS)
    block_start = jnp.arange(nblocks, dtype=jnp.int32) * MOE_ROWS
    block_expert = jnp.minimum(jnp.sum(rend[None, :] <= block_start[:, None], axis=1), n_experts - 1)
    n_used = (rend[-1] // MOE_ROWS).reshape(1)
    i32 = lambda v: v.astype(jnp.int32)
    return (i32(base.reshape(-1)), i32(cnt.reshape(-1)), i32(npass), i32(rend), i32(block_expert), i32(n_used),
            nblocks * MOE_ROWS)


def _combine_kernel(base_ref, cnt_ref, npass_ref, te_ref, gt_ref, h1_ref, g_ref, ys_hbm, o_ref, ybuf, acc, sem):
    b = pl.program_id(0)
    nb = pl.num_programs(0)
    n_experts = ybuf.shape[1] // (SLAB * (h1_ref.shape[1] // 128))
    lt = h1_ref.shape[1] // 128
    srows = SLAB * lt
    slot = b % 2

    def slab_copy(blk, sl, e, p, full):
        nrows = srows if full else SLAB_SHORT * lt
        src = pl.multiple_of((base_ref[blk * n_experts + e] + p * SLAB) * lt, lt)
        return pltpu.make_async_copy(ys_hbm.at[pl.ds(src, nrows), :], ybuf.at[sl, pl.ds(e * srows, nrows), :],
                                     sem.at[sl, int(full)])

    def fetch(blk, sl):
        for e in range(n_experts):
            full = cnt_ref[blk * n_experts + e] > SLAB_SHORT

            @pl.when(full)
            def _():
                slab_copy(blk, sl, e, 0, True).start()

            @pl.when(jnp.logical_not(full))
            def _():
                slab_copy(blk, sl, e, 0, False).start()

    @pl.when(b == 0)
    def _():
        ybuf[...] = jnp.zeros(ybuf.shape, F32)
        fetch(b, slot)

    @pl.when(b + 1 < nb)
    def _():
        fetch(b + 1, 1 - slot)

    n_full = jnp.int32(0)
    for e in range(n_experts):
        n_full = n_full + (cnt_ref[b * n_experts + e] > SLAB_SHORT).astype(jnp.int32)

    @pl.loop(0, n_full)
    def _(_):
        slab_copy(b, slot, 0, 0, True).wait()

    @pl.loop(0, n_experts - n_full)
    def _(_):
        slab_copy(b, slot, 0, 0, False).wait()

    def gathered(p):
        te = te_ref[...]
        sel, ind = _rank_one_hot(te, n_experts, (p * SLAB).astype(F32))
        eio = lax.broadcasted_iota(jnp.int32, ind.shape, 0)
        gate_te = jnp.zeros(ind.shape, F32)
        for kk in range(TOP_K):
            gate_te = gate_te + jnp.where(te[kk:kk + 1, :] == eio, gt_ref[kk:kk + 1, :], 0.0)
        gate_rows = jnp.concatenate(
            [jnp.sum(jnp.where(sel[e * SLAB:(e + 1) * SLAB], gate_te[e:e + 1, :], 0.0), axis=1, keepdims=True)
             for e in range(n_experts)], axis=0)
        y = jnp.concatenate([ybuf[slot, pl.ds(c, n_experts * SLAB, stride=lt), :] for c in range(lt)], axis=1)
        y = y * gate_rows
        hi = y.astype(BF16)
        lo = (y - hi.astype(F32)).astype(BF16)
        tn_dims = (((0,), (0,)), ((), ()))
        onehot = sel.astype(BF16)
        return (lax.dot_general(onehot, hi, tn_dims, preferred_element_type=F32)
                + lax.dot_general(onehot, lo, tn_dims, preferred_element_type=F32))

    acc[...] = h1_ref[...] + gathered(jnp.int32(0))

    @pl.loop(1, npass_ref[b])
    def _(p):
        for e in range(n_experts):
            @pl.when(cnt_ref[b * n_experts + e] > p * SLAB)
            def _():
                cp = slab_copy(b, slot, e, p, True)
                cp.start()
                cp.wait()
        acc[...] += gathered(p)

    o_ref[...] = _rms(acc[...], g_ref[...])


def _combine(h1, top_et, gates_t, ys, base, cnt, npass, g_final, nreal, n_experts):
    d = h1.shape[1]
    lt = d // 128
    grid_spec = pltpu.PrefetchScalarGridSpec(
        num_scalar_prefetch=3,
        grid=(nreal // TOK_BLOCK,),
        in_specs=[pl.BlockSpec((TOP_K, TOK_BLOCK), lambda b, *_: (0, b)),
                  pl.BlockSpec((TOP_K, TOK_BLOCK), lambda b, *_: (0, b)),
                  pl.BlockSpec((TOK_BLOCK, d), lambda b, *_: (b, 0)),
                  pl.BlockSpec((1, d), lambda b, *_: (0, 0)),
                  pl.BlockSpec(memory_space=pl.ANY)],
        out_specs=pl.BlockSpec((TOK_BLOCK, d), lambda b, *_: (b, 0)),
        scratch_shapes=[pltpu.VMEM((2, n_experts * SLAB * lt, 128), F32),
                        pltpu.VMEM((TOK_BLOCK, d), F32),
                        pltpu.SemaphoreType.DMA((2, 2))],
    )
    return pl.pallas_call(
        _combine_kernel,
        grid_spec=grid_spec,
        out_shape=jax.ShapeDtypeStruct((nreal, d), F32),
        compiler_params=pltpu.CompilerParams(dimension_semantics=("arbitrary",), vmem_limit_bytes=VMEM_LIMIT),
        name="moe_combine",
    )(base, cnt, npass, top_et, gates_t, h1, g_final, ys)


def kernel(x, meta_tokens, g_mix, w_in, g_attn_out, conv_w, conv_b, g_conv_out, w_out, g_ffn,
           w_router, b_router, w_gu, b_gu, w_down, b_down, g_final):
    b, seq, d = x.shape
    depth = w_in.shape[0]
    assert depth == 1 and seq % ROW_BLOCK == 0
    a = w_out.shape[1] // 2
    n_experts = w_router.shape[2]

    meta_blk = jnp.concatenate([jnp.zeros((ROW_BLOCK - N_META, d), x.dtype), meta_tokens], axis=0)
    w = w_in[0]
    scale = HEAD_DIM ** -0.5 * LOG2_E
    wqt = (w[:, :a] * scale).T.astype(BF16)
    wk = w[:, a:2 * a].astype(BF16)
    wvt = w[:, 2 * a:3 * a].T.astype(BF16)
    wc = w[:, 3 * a:].astype(BF16)
    qt, k5, v5, convn = _in_proj(x, meta_blk, g_mix, wqt, wk, wvt, wc, conv_w[0], conv_b, g_conv_out)
    att_t = _attention(qt, k5, v5)

    wo = w_out[0].astype(BF16)
    h1, xn, top_e, gates = _post(att_t, convn, x, meta_tokens, g_attn_out, wo[:a], wo[a:], g_ffn,
                                 w_router[0], b_router)

    base, cnt, npass, rend, block_expert, n_used, cap = _route_tables(top_e, n_experts)
    top_et, gates_t = top_e.T, gates.T
    xs = _dispatch(top_et, xn, base, cnt, npass, rend, cap)
    ys = _ffn(xs, block_expert, n_used, w_gu[0], b_gu[0], w_down[0], b_down[0])
    out = _combine(h1, top_et, gates_t, ys, base, cnt, npass, g_final.reshape(1, d), b * seq, n_experts)
    return out.reshape(b, seq, d)
```

```python
import functools

import jax
import jax.numpy as jnp
from jax import lax
from jax.experimental import pallas as pl
from jax.experimental.pallas import tpu as pltpu

N_META = 16
HEAD_DIM = 64
CONV_K = 3
TOP_K = 4
SWIGLU_LIMIT = 7.0
SWIGLU_ALPHA = 1.702
RMS_EPS = 1e-6

SUBLANES = 8
ROW_BLOCK = 256
KEY_CHUNK = 128
KEY_GROUPS = KEY_CHUNK // SUBLANES
ATTN_CHUNKS_PER_ITER = ROW_BLOCK // KEY_CHUNK
ATTN_HEADS_PER_STEP = 2
STICK_DEAD_BITS = 160.0
LOG2_E = 1.4426950408889634
MOE_ROWS = 256
TOK_BLOCK = 128
SLAB = 32
SLAB_SHORT = 24
CARRY_ROWS = SUBLANES
VMEM_LIMIT = 48 * 1024 * 1024

F32 = jnp.float32
BF16 = jnp.bfloat16


def _rms(x, g):
    return x * lax.rsqrt(jnp.mean(x * x, axis=-1, keepdims=True) + RMS_EPS) * g


def _in_proj_kernel(x_ref, meta_ref, g_ref, perm_ref, wqt_ref, wk_ref, wvt_ref, wc_ref, cw_ref, cb_ref, gc_ref,
                    qt_ref, k_ref, vt_ref, conv_ref, u_scr):
    j = pl.program_id(1)
    rows = x_ref.shape[1]
    cw = conv_ref.shape[2]
    h = jnp.where(j == 0, meta_ref[...], x_ref[0])
    nb = _rms(h, g_ref[...]).astype(BF16)
    nt_dims = (((1,), (1,)), ((), ()))
    qt_ref[0, 0] = lax.dot_general(wqt_ref[...], nb, nt_dims, preferred_element_type=F32).astype(BF16)
    nperm = jnp.dot(perm_ref[...], nb, preferred_element_type=F32).astype(BF16)
    kp = jnp.dot(nperm, wk_ref[...], preferred_element_type=F32).astype(BF16)
    vtp = lax.dot_general(wvt_ref[...], nperm, nt_dims, preferred_element_type=F32).astype(BF16)
    for hh in range(k_ref.shape[1]):
        for c in range(rows // KEY_CHUNK):
            k_ref[0, hh, c] = kp[c * KEY_CHUNK:(c + 1) * KEY_CHUNK, hh * HEAD_DIM:(hh + 1) * HEAD_DIM]
            vt_ref[0, hh, c] = vtp[hh * HEAD_DIM:(hh + 1) * HEAD_DIM, c * KEY_CHUNK:(c + 1) * KEY_CHUNK]
    cp = jnp.dot(nb, wc_ref[...], preferred_element_type=F32)
    gate_b, u = cp[:, :cw], cp[:, cw:2 * cw] * cp[:, 2 * cw:]

    @pl.when(j == 0)
    def _():
        u_scr[0:CARRY_ROWS, :] = jnp.zeros((CARRY_ROWS, cw), F32)

    u_scr[CARRY_ROWS:CARRY_ROWS + rows, :] = u
    u1 = u_scr[CARRY_ROWS - 1:CARRY_ROWS - 1 + rows, :]
    u2 = u_scr[CARRY_ROWS - 2:CARRY_ROWS - 2 + rows, :]
    y = cw_ref[0:1, :] * u2 + cw_ref[1:2, :] * u1 + cw_ref[2:3, :] * u
    u_scr[0:CARRY_ROWS, :] = u_scr[rows:rows + CARRY_ROWS, :]
    conv = gate_b * (y + cb_ref[...])
    conv_ref[0] = _rms(conv, gc_ref[...]).astype(BF16)


def _key_order_permutation():
    p = jnp.arange(ROW_BLOCK)
    chunk, g, r = p // KEY_CHUNK, (p % KEY_CHUNK) // SUBLANES, p % SUBLANES
    src = chunk * KEY_CHUNK + r * KEY_GROUPS + g
    return (src[:, None] == jnp.arange(ROW_BLOCK)[None, :]).astype(BF16)


def _in_proj(x, meta_blk, g_mix, wqt, wk, wvt, wc, conv_w, conv_b, g_conv):
    b, seq, d = x.shape
    a = wk.shape[1]
    cw = wc.shape[1] // 3
    heads = a // HEAD_DIM
    nblk = seq // ROW_BLOCK + 1
    tp = nblk * ROW_BLOCK
    cpb = ROW_BLOCK // KEY_CHUNK
    full = lambda shp: pl.BlockSpec(shp, lambda bi, j: (0,) * len(shp))
    return pl.pallas_call(
        _in_proj_kernel,
        grid=(b, nblk),
        in_specs=[
            pl.BlockSpec((1, ROW_BLOCK, d), lambda bi, j: (bi, jnp.maximum(j - 1, 0), 0)),
            full((ROW_BLOCK, d)), full((1, d)), full((ROW_BLOCK, ROW_BLOCK)),
            full((a, d)), full((d, a)), full((a, d)), full((d, 3 * cw)),
            full((CONV_K, cw)), full((1, cw)), full((1, cw)),
        ],
        out_specs=[
            pl.BlockSpec((1, 1, a, ROW_BLOCK), lambda bi, j: (bi, j, 0, 0)),
            pl.BlockSpec((1, heads, cpb, KEY_CHUNK, HEAD_DIM), lambda bi, j: (bi, 0, j, 0, 0)),
            pl.BlockSpec((1, heads, cpb, HEAD_DIM, KEY_CHUNK), lambda bi, j: (bi, 0, j, 0, 0)),
            pl.BlockSpec((1, ROW_BLOCK, cw), lambda bi, j: (bi, j, 0)),
        ],
        out_shape=[
            jax.ShapeDtypeStruct((b, nblk, a, ROW_BLOCK), BF16),
            jax.ShapeDtypeStruct((b, heads, tp // KEY_CHUNK, KEY_CHUNK, HEAD_DIM), BF16),
            jax.ShapeDtypeStruct((b, heads, tp // KEY_CHUNK, HEAD_DIM, KEY_CHUNK), BF16),
            jax.ShapeDtypeStruct((b, tp, cw), BF16),
        ],
        scratch_shapes=[pltpu.VMEM((CARRY_ROWS + ROW_BLOCK, cw), F32)],
        compiler_params=pltpu.CompilerParams(
            dimension_semantics=("parallel", "arbitrary"), vmem_limit_bytes=VMEM_LIMIT),
        name="in_proj",
    )(x, meta_blk, g_mix, _key_order_permutation(), wqt, wk, wvt, wc, conv_w, conv_b, g_conv)


def _attn_kernel(qt_ref, k_ref, vt_ref, o_ref, z_scr, zd_scr, w_scr):
    nblk, nq = qt_ref.shape[1], qt_ref.shape[3]
    hd = HEAD_DIM
    heads = range(ATTN_HEADS_PER_STEP)
    per_iter = ATTN_CHUNKS_PER_ITER
    shape3 = (KEY_GROUPS, SUBLANES, nq)
    key_in_chunk = (lax.broadcasted_iota(jnp.int32, shape3, 1) * KEY_GROUPS
                    + lax.broadcasted_iota(jnp.int32, shape3, 0))
    lane = lax.broadcasted_iota(jnp.int32, shape3, 2)
    sub = lax.broadcasted_iota(jnp.int32, (SUBLANES, nq), 0)
    step_top = lambda i, t: (i + 1 - t) * per_iter - 1

    def scores(dst, i, jtop):
        for h in heads:
            qt = qt_ref[0, i, h * hd:(h + 1) * hd, :]
            for c in range(per_iter):
                dst[h, c] = jnp.dot(k_ref[0, h, jtop - c], qt, preferred_element_type=F32)

    def weights(src, carries, i, jtop, masked):
        return tuple(head_weights(src, h, carries[h], i, jtop, masked) for h in heads)

    def head_weights(src, h, carry, i, jtop, masked):
        for c in range(per_iter):
            z = src[h, c].reshape(shape3)
            sp = jnp.maximum(z, jnp.log(1.0 + jnp.exp2(jnp.minimum(z, 64.0))) * LOG2_E)
            if masked:
                valid = (jtop - c) * KEY_CHUNK + key_in_chunk < i * nq + lane
                sp = jnp.where(valid, sp, 0.0)
            runs = [None] * KEY_GROUPS
            run = sp[KEY_GROUPS - 1]
            runs[KEY_GROUPS - 1] = run
            for g in range(KEY_GROUPS - 2, -1, -1):
                run = run + sp[g]
                runs[g] = run
            tail = run
            for shift in (1, 2, 4):
                tail = tail + jnp.where(sub < SUBLANES - shift, pltpu.roll(tail, SUBLANES - shift, 0), 0.0)
            off = carry + (tail - run)
            w = jnp.exp2(z - (jnp.stack(runs, axis=0) + off[None]))
            if masked:
                w = jnp.where(valid, w, 0.0)
            w_scr[h, c] = w.reshape(KEY_CHUNK, nq).astype(BF16)
            carry = carry + tail[0:1, :]
        return carry

    def weighted_values(accs, jtop):
        accs = list(accs)
        for h in heads:
            for c in range(per_iter):
                accs[h] = accs[h] + jnp.dot(vt_ref[0, h, jtop - c], w_scr[h, c], preferred_element_type=F32)
        return tuple(accs)

    def store(i, accs):
        for h in heads:
            o_ref[0, i, h * hd:(h + 1) * hd, :] = accs[h]

    zero_carry = tuple(jnp.zeros((1, nq), F32) for _ in heads)
    zero_acc = tuple(jnp.zeros((hd, nq), F32) for _ in heads)
    w_scr[...] = jnp.zeros(w_scr.shape, BF16)
    scores(zd_scr, 0, step_top(0, 0))

    def stick_left(carries):
        m = carries[0]
        for h in heads[1:]:
            m = jnp.minimum(m, carries[h])
        return jnp.min(m) < STICK_DEAD_BITS

    def query_block(i, state):
        accs, pending_top = state
        store(jnp.maximum(i - 1, 0), weighted_values(accs, pending_top))
        carries = weights(zd_scr, zero_carry, i, step_top(i, 0), True)
        nxt = jnp.minimum(i + 1, nblk - 1)
        scores(zd_scr, nxt, step_top(nxt, 0))
        scores(z_scr, i, jnp.maximum(step_top(i, 1), per_iter - 1))

        def step(state):
            t, _, carries, accs = state
            live = stick_left(carries)
            accs = weighted_values(accs, step_top(i, t - 1))
            carries = weights(z_scr, carries, i, step_top(i, t), False)
            scores(z_scr, i, jnp.maximum(step_top(i, t + 1), per_iter - 1))
            return t + 1, live, carries, accs

        t_end, _, _, accs = lax.while_loop(lambda s: (s[0] <= i) & s[1], step,
                                           (jnp.int32(1), jnp.bool_(True), carries, zero_acc))
        return accs, step_top(i, t_end - 1)

    accs, pending_top = lax.fori_loop(0, nblk, query_block, (zero_acc, jnp.int32(per_iter - 1)))
    store(nblk - 1, weighted_values(accs, pending_top))


def _attention(qt, k5, v5):
    b, nblk, a, nq = qt.shape
    heads = a // HEAD_DIM
    nchunks = k5.shape[2]
    hps = ATTN_HEADS_PER_STEP
    return pl.pallas_call(
        _attn_kernel,
        grid=(b, heads // hps),
        in_specs=[
            pl.BlockSpec((1, nblk, hps * HEAD_DIM, nq), lambda bi, h: (bi, 0, h, 0)),
            pl.BlockSpec((1, hps, nchunks, KEY_CHUNK, HEAD_DIM), lambda bi, h: (bi, h, 0, 0, 0)),
            pl.BlockSpec((1, hps, nchunks, HEAD_DIM, KEY_CHUNK), lambda bi, h: (bi, h, 0, 0, 0)),
        ],
        out_specs=pl.BlockSpec((1, nblk, hps * HEAD_DIM, nq), lambda bi, h: (bi, 0, h, 0)),
        out_shape=jax.ShapeDtypeStruct((b, nblk, a, nq), F32),
        scratch_shapes=[pltpu.VMEM((hps, ATTN_CHUNKS_PER_ITER, KEY_CHUNK, nq), F32),
                        pltpu.VMEM((hps, ATTN_CHUNKS_PER_ITER, KEY_CHUNK, nq), F32),
                        pltpu.VMEM((hps, ATTN_CHUNKS_PER_ITER, KEY_CHUNK, nq), BF16)],
        compiler_params=pltpu.CompilerParams(
            dimension_semantics=("parallel", "parallel"), vmem_limit_bytes=VMEM_LIMIT),
        name="stickbreak_attn",
    )(qt, k5, v5)


def _top_k_gates(logits):
    n, e = logits.shape
    lane = lax.broadcasted_iota(jnp.int32, (n, e), 1)
    slot = lax.broadcasted_iota(jnp.int32, (n, TOP_K), 1)
    top_e = jnp.zeros((n, TOP_K), jnp.int32)
    top_l = jnp.zeros((n, TOP_K), F32)
    cur = logits
    for kk in range(TOP_K):
        m = jnp.max(cur, axis=-1, keepdims=True)
        idx = jnp.min(jnp.where(cur == m, lane, e), axis=-1, keepdims=True)
        top_e = jnp.where(slot == kk, idx, top_e)
        top_l = jnp.where(slot == kk, m, top_l)
        cur = jnp.where(lane == idx, -jnp.inf, cur)
    ex = jnp.exp(top_l - top_l[:, 0:1])
    return top_e, ex / jnp.sum(ex, axis=-1, keepdims=True)


def _post_kernel(at_ref, conv_ref, h_ref, ga_ref, wa_ref, wcv_ref, gf_ref, wr_ref, br_ref, *rest, meta):
    h1_ref, xn_ref, te_ref, gt_ref = rest[-4:]
    att = at_ref[0, 0].T
    convn = conv_ref[0]
    h = h_ref[...] if meta else h_ref[0]
    if meta:
        att, convn = att[ROW_BLOCK - N_META:, :], convn[ROW_BLOCK - N_META:, :]
    attn = _rms(att, ga_ref[...]).astype(BF16)
    h1 = (h + jnp.dot(attn, wa_ref[...], preferred_element_type=F32)
          + jnp.dot(convn, wcv_ref[...], preferred_element_type=F32))
    xn = _rms(h1, gf_ref[...])
    xh = xn.astype(BF16)
    xl = (xn - xh.astype(F32)).astype(BF16)
    logits = (jnp.dot(xh, wr_ref[0], preferred_element_type=F32) + jnp.dot(xl, wr_ref[0], preferred_element_type=F32)
              + jnp.dot(xh, wr_ref[1], preferred_element_type=F32) + br_ref[...])
    top_e, gates = _top_k_gates(logits)
    h1_ref[...] = h1
    xn_ref[...] = xh
    te_ref[...] = top_e
    gt_ref[...] = gates


def _post(att_t, convn, x, meta_tokens, g_attn, w_out_a, w_out_c, g_ffn, w_router, b_router):
    b, seq, d = x.shape
    a = att_t.shape[2]
    cw = convn.shape[2]
    nreal = b * seq
    ntok = nreal + b * N_META
    nblk = seq // ROW_BLOCK
    out_shape = [
        jax.ShapeDtypeStruct((ntok, d), F32),
        jax.ShapeDtypeStruct((ntok, d), BF16),
        jax.ShapeDtypeStruct((ntok, TOP_K), jnp.int32),
        jax.ShapeDtypeStruct((ntok, TOP_K), F32),
    ]
    weights = (g_attn, w_out_a, w_out_c, g_ffn, w_router, b_router)

    def wspecs(nargs):
        return [pl.BlockSpec(w.shape, lambda *idx, n=w.ndim: (0,) * n) for w in weights]

    def ospecs(rows, index):
        return [pl.BlockSpec((rows, d), index), pl.BlockSpec((rows, d), index),
                pl.BlockSpec((rows, TOP_K), index), pl.BlockSpec((rows, TOP_K), index)]

    real = pl.pallas_call(
        functools.partial(_post_kernel, meta=False),
        grid=(b, nblk),
        in_specs=[
            pl.BlockSpec((1, 1, a, ROW_BLOCK), lambda bi, j: (bi, j + 1, 0, 0)),
            pl.BlockSpec((1, ROW_BLOCK, cw), lambda bi, j: (bi, j + 1, 0)),
            pl.BlockSpec((1, ROW_BLOCK, d), lambda bi, j: (bi, j, 0)),
        ] + wspecs(2),
        out_specs=ospecs(ROW_BLOCK, lambda bi, j: (bi * nblk + j, 0)),
        out_shape=out_shape,
        compiler_params=pltpu.CompilerParams(
            dimension_semantics=("parallel", "parallel"), vmem_limit_bytes=VMEM_LIMIT),
        name="post_attn_real",
    )(att_t, convn, x, *weights)

    meta_blk0 = nreal // N_META
    return pl.pallas_call(
        functools.partial(_post_kernel, meta=True),
        grid=(b,),
        in_specs=[
            pl.BlockSpec((1, 1, a, ROW_BLOCK), lambda bi: (bi, 0, 0, 0)),
            pl.BlockSpec((1, ROW_BLOCK, cw), lambda bi: (bi, 0, 0)),
            pl.BlockSpec((N_META, d), lambda bi: (0, 0)),
        ] + wspecs(1) + [pl.BlockSpec(memory_space=pl.ANY)] * 4,
        out_specs=ospecs(N_META, lambda bi: (meta_blk0 + bi, 0)),
        out_shape=out_shape,
        input_output_aliases={9: 0, 10: 1, 11: 2, 12: 3},
        compiler_params=pltpu.CompilerParams(
            dimension_semantics=("parallel",), vmem_limit_bytes=VMEM_LIMIT),
        name="post_attn_meta",
    )(att_t, convn, meta_tokens, *weights, *real)


def _rank_one_hot(te, n_experts, shift):
    t = te.shape[1]
    eio = lax.broadcasted_iota(jnp.int32, (n_experts, t), 0)
    ind = te[0:1, :] == eio
    for kk in range(1, TOP_K):
        ind = ind | (te[kk:kk + 1, :] == eio)
    before = (lax.broadcasted_iota(jnp.int32, (t, t), 0) < lax.broadcasted_iota(jnp.int32, (t, t), 1)).astype(BF16)
    rank = jnp.dot(ind.astype(BF16), before, preferred_element_type=F32)
    rank = jnp.where(ind, rank - shift, -1.0)
    rio = lax.broadcasted_iota(jnp.int32, (SLAB, t), 0).astype(F32)
    sel = [jnp.broadcast_to(rank[e:e + 1, :], (SLAB, t)) == rio for e in range(n_experts)]
    return jnp.concatenate(sel, axis=0), ind


def _dispatch_kernel(base_ref, cnt_ref, npass_ref, rend_ref, te_ref, xn_ref, xs_hbm, slab, zbuf, sem, pend):
    b = pl.program_id(0)
    n_experts = rend_ref.shape[0]
    lt = xn_ref.shape[1] // 128
    srows = SLAB * lt
    slot = b % 2

    def slab_copy(sl, e, p, full):
        nrows = srows if full else SLAB_SHORT * lt
        dst = pl.multiple_of((base_ref[b * n_experts + e] + p * SLAB) * lt, lt)
        return pltpu.make_async_copy(slab.at[sl, pl.ds(e * srows, nrows), :], xs_hbm.at[pl.ds(dst, nrows), :],
                                     sem.at[sl, int(full)])

    def wait_slot(sl):
        for full in (False, True):
            @pl.loop(0, pend[sl, int(full)])
            def _(_):
                slab_copy(sl, 0, 0, full).wait()
            pend[sl, int(full)] = 0

    @pl.when(b == 0)
    def _():
        zbuf[...] = jnp.zeros(zbuf.shape, F32)
        zrows = zbuf.shape[0]

        def clear(e):
            dst = pl.multiple_of(jnp.maximum(rend_ref[e] * lt - zrows, 0), lt)
            return pltpu.make_async_copy(zbuf, xs_hbm.at[pl.ds(dst, zrows), :], sem.at[0, 0])

        for e in range(n_experts):
            clear(e).start()
            clear(e).wait()
        for sl in range(2):
            pend[sl, 0] = 0
            pend[sl, 1] = 0

    def fill(sl, p):
        sel, _ = _rank_one_hot(te_ref[...], n_experts, (p * SLAB).astype(F32))
        rows = jnp.dot(sel.astype(BF16), xn_ref[...], preferred_element_type=F32)
        for c in range(lt):
            slab[sl, pl.ds(c, n_experts * SLAB, stride=lt), :] = rows[:, c * 128:(c + 1) * 128]

    fill(slot, jnp.int32(0))
    wait_slot(1 - slot)
    n_full = jnp.int32(0)
    for e in range(n_experts):
        full = cnt_ref[b * n_experts + e] > SLAB_SHORT
        n_full = n_full + full.astype(jnp.int32)

        @pl.when(full)
        def _():
            slab_copy(slot, e, 0, True).start()

        @pl.when(jnp.logical_not(full))
        def _():
            slab_copy(slot, e, 0, False).start()
    pend[slot, 1] = n_full
    pend[slot, 0] = n_experts - n_full

    @pl.loop(1, npass_ref[b])
    def _(p):
        wait_slot(slot)
        fill(slot, p)
        for e in range(n_experts):
            @pl.when(cnt_ref[b * n_experts + e] > p * SLAB)
            def _():
                cp = slab_copy(slot, e, p, True)
                cp.start()
                cp.wait()

    @pl.when(b == pl.num_programs(0) - 1)
    def _():
        wait_slot(slot)
        wait_slot(1 - slot)


def _dispatch(top_et, xn, base, cnt, npass, rend, cap):
    ntok, d = xn.shape
    n_experts = rend.shape[0]
    lt = d // 128
    assert d % 128 == 0 and lt % SUBLANES == 0 and ntok % TOK_BLOCK == 0
    grid_spec = pltpu.PrefetchScalarGridSpec(
        num_scalar_prefetch=4,
        grid=(ntok // TOK_BLOCK,),
        in_specs=[pl.BlockSpec((TOP_K, TOK_BLOCK), lambda b, *_: (0, b)),
                  pl.BlockSpec((TOK_BLOCK, d), lambda b, *_: (b, 0))],
        out_specs=pl.BlockSpec(memory_space=pl.ANY),
        scratch_shapes=[pltpu.VMEM((2, n_experts * SLAB * lt, 128), F32),
                        pltpu.VMEM((2 * MOE_ROWS * lt, 128), F32),
                        pltpu.SemaphoreType.DMA((2, 2)),
                        pltpu.SMEM((2, 2), jnp.int32)],
    )
    return pl.pallas_call(
        _dispatch_kernel,
        grid_spec=grid_spec,
        out_shape=jax.ShapeDtypeStruct((cap * lt, 128), F32),
        compiler_params=pltpu.CompilerParams(dimension_semantics=("arbitrary",), vmem_limit_bytes=VMEM_LIMIT),
        name="moe_dispatch",
    )(base, cnt, npass, rend, top_et, xn)


def _ffn_kernel(bexp_ref, nused_ref, xs_ref, wgu_ref, bgu_ref, wd_ref, bd_ref, ys_ref, wgu_bf, wd_bf):
    i = pl.program_id(0)
    ff, d = wd_bf.shape
    lt = d // 128

    @pl.when(i < nused_ref[0])
    def _():
        @pl.when((i == 0) | (bexp_ref[i] != bexp_ref[jnp.maximum(i - 1, 0)]))
        def _():
            wgu_bf[...] = wgu_ref[0].astype(BF16)
            wd_bf[...] = wd_ref[0].astype(BF16)

        x = jnp.concatenate([xs_ref[pl.ds(c, MOE_ROWS, stride=lt), :] for c in range(lt)], axis=1).astype(BF16)
        gu = jnp.dot(x, wgu_bf[...], preferred_element_type=F32) + bgu_ref[0]
        gate = jnp.minimum(gu[:, :ff], SWIGLU_LIMIT)
        lin = jnp.clip(gu[:, ff:], -SWIGLU_LIMIT, SWIGLU_LIMIT)
        hid = (lin + 1.0) * (gate * jax.nn.sigmoid(SWIGLU_ALPHA * gate))
        y = jnp.dot(hid.astype(BF16), wd_bf[...], preferred_element_type=F32) + bd_ref[0]
        for c in range(lt):
            ys_ref[pl.ds(c, MOE_ROWS, stride=lt), :] = y[:, c * 128:(c + 1) * 128]

    @pl.when(i >= nused_ref[0])
    def _():
        ys_ref[...] = jnp.zeros(ys_ref.shape, F32)


def _ffn(xs, block_expert, n_used, w_gu, b_gu, w_down, b_down):
    e, d, ff2 = w_gu.shape
    ff = ff2 // 2
    lt = d // 128
    nblocks = block_expert.shape[0]
    rows = MOE_ROWS * lt
    grid_spec = pltpu.PrefetchScalarGridSpec(
        num_scalar_prefetch=2,
        grid=(nblocks,),
        in_specs=[
            pl.BlockSpec((rows, 128), lambda i, be, nu: (i, 0)),
            pl.BlockSpec((1, d, ff2), lambda i, be, nu: (be[i], 0, 0)),
            pl.BlockSpec((1, 1, ff2), lambda i, be, nu: (be[i], 0, 0)),
            pl.BlockSpec((1, ff, d), lambda i, be, nu: (be[i], 0, 0)),
            pl.BlockSpec((1, 1, d), lambda i, be, nu: (be[i], 0, 0)),
        ],
        out_specs=pl.BlockSpec((rows, 128), lambda i, be, nu: (i, 0)),
        scratch_shapes=[pltpu.VMEM((d, ff2), BF16), pltpu.VMEM((ff, d), BF16)],
    )
    return pl.pallas_call(
        _ffn_kernel,
        grid_spec=grid_spec,
        out_shape=jax.ShapeDtypeStruct(xs.shape, F32),
        compiler_params=pltpu.CompilerParams(dimension_semantics=("arbitrary",), vmem_limit_bytes=VMEM_LIMIT),
        name="moe_experts",
    )(block_expert, n_used, xs, w_gu, b_gu.reshape(e, 1, ff2), w_down, b_down.reshape(e, 1, d))


def _route_tables(top_e, n_experts):
    ntok = top_e.shape[0]
    nb = ntok // TOK_BLOCK
    ids = jnp.arange(n_experts, dtype=jnp.int32)
    cnt = jnp.sum(top_e.reshape(nb, TOK_BLOCK * TOP_K, 1) == ids, axis=1, dtype=jnp.int32)
    total = jnp.sum(cnt, axis=0)
    region = (total + SLAB + MOE_ROWS - 1) // MOE_ROWS * MOE_ROWS
    rend = jnp.cumsum(region)
    base = (rend - region)[None, :] + jnp.cumsum(cnt, axis=0) - cnt
    npass = jnp.maximum((jnp.max(cnt, axis=1) + SLAB - 1) // SLAB, 1)
    nblocks = -(-(ntok * TOP_K + n_experts * (SLAB + MOE_ROWS - 1)) // MOE_ROWS)
    block_start = jnp.arange(nblocks, dtype=jnp.int32) * MOE_ROWS
    block_expert = jnp.minimum(jnp.sum(rend[None, :] <= block_start[:, None], axis=1), n_experts - 1)
    n_used = (rend[-1] // MOE_ROWS).reshape(1)
    i32 = lambda v: v.astype(jnp.int32)
    return (i32(base.reshape(-1)), i32(cnt.reshape(-1)), i32(npass), i32(rend), i32(block_expert), i32(n_used),
            nblocks * MOE_ROWS)


def _combine_kernel(base_ref, cnt_ref, npass_ref, te_ref, gt_ref, h1_ref, g_ref, ys_hbm, o_ref, ybuf, acc, sem):
    b = pl.program_id(0)
    nb = pl.num_programs(0)
    n_experts = ybuf.shape[1] // (SLAB * (h1_ref.shape[1] // 128))
    lt = h1_ref.shape[1] // 128
    srows = SLAB * lt
    slot = b % 2

    def slab_copy(blk, sl, e, p):
        src = pl.multiple_of((base_ref[blk * n_experts + e] + p * SLAB) * lt, lt)
        return pltpu.make_async_copy(ys_hbm.at[pl.ds(src, srows), :], ybuf.at[sl, pl.ds(e * srows, srows), :],
                                     sem.at[sl])

    def fetch(blk, sl):
        for e in range(n_experts):
            slab_copy(blk, sl, e, 0).start()

    @pl.when(b == 0)
    def _():
        fetch(b, slot)

    @pl.when(b + 1 < nb)
    def _():
        fetch(b + 1, 1 - slot)

    for e in range(n_experts):
        slab_copy(b, slot, e, 0).wait()

    def gathered(p):
        te = te_ref[...]
        sel, ind = _rank_one_hot(te, n_experts, (p * SLAB).astype(F32))
        eio = lax.broadcasted_iota(jnp.int32, ind.shape, 0)
        gate_te = jnp.zeros(ind.shape, F32)
        for kk in range(TOP_K):
            gate_te = gate_te + jnp.where(te[kk:kk + 1, :] == eio, gt_ref[kk:kk + 1, :], 0.0)
        gate_rows = jnp.concatenate(
            [jnp.sum(jnp.where(sel[e * SLAB:(e + 1) * SLAB], gate_te[e:e + 1, :], 0.0), axis=1, keepdims=True)
             for e in range(n_experts)], axis=0)
        y = jnp.concatenate([ybuf[slot, pl.ds(c, n_experts * SLAB, stride=lt), :] for c in range(lt)], axis=1)
        y = y * gate_rows
        hi = y.astype(BF16)
        lo = (y - hi.astype(F32)).astype(BF16)
        tn_dims = (((0,), (0,)), ((), ()))
        onehot = sel.astype(BF16)
        return (lax.dot_general(onehot, hi, tn_dims, preferred_element_type=F32)
                + lax.dot_general(onehot, lo, tn_dims, preferred_element_type=F32))

    acc[...] = h1_ref[...] + gathered(jnp.int32(0))

    @pl.loop(1, npass_ref[b])
    def _(p):
        for e in range(n_experts):
            @pl.when(cnt_ref[b * n_experts + e] > p * SLAB)
            def _():
                cp = slab_copy(b, slot, e, p)
                cp.start()
                cp.wait()
        acc[...] += gathered(p)

    o_ref[...] = _rms(acc[...], g_ref[...])


def _combine(h1, top_et, gates_t, ys, base, cnt, npass, g_final, nreal, n_experts):
    d = h1.shape[1]
    lt = d // 128
    grid_spec = pltpu.PrefetchScalarGridSpec(
        num_scalar_prefetch=3,
        grid=(nreal // TOK_BLOCK,),
        in_specs=[pl.BlockSpec((TOP_K, TOK_BLOCK), lambda b, *_: (0, b)),
                  pl.BlockSpec((TOP_K, TOK_BLOCK), lambda b, *_: (0, b)),
                  pl.BlockSpec((TOK_BLOCK, d), lambda b, *_: (b, 0)),
                  pl.BlockSpec((1, d), lambda b, *_: (0, 0)),
                  pl.BlockSpec(memory_space=pl.ANY)],
        out_specs=pl.BlockSpec((TOK_BLOCK, d), lambda b, *_: (b, 0)),
        scratch_shapes=[pltpu.VMEM((2, n_experts * SLAB * lt, 128), F32),
                        pltpu.VMEM((TOK_BLOCK, d), F32),
                        pltpu.SemaphoreType.DMA((2,))],
    )
    return pl.pallas_call(
        _combine_kernel,
        grid_spec=grid_spec,
        out_shape=jax.ShapeDtypeStruct((nreal, d), F32),
        compiler_params=pltpu.CompilerParams(dimension_semantics=("arbitrary",), vmem_limit_bytes=VMEM_LIMIT),
        name="moe_combine",
    )(base, cnt, npass, top_et, gates_t, h1, g_final, ys)


def kernel(x, meta_tokens, g_mix, w_in, g_attn_out, conv_w, conv_b, g_conv_out, w_out, g_ffn,
           w_router, b_router, w_gu, b_gu, w_down, b_down, g_final):
    b, seq, d = x.shape
    depth = w_in.shape[0]
    assert depth == 1 and seq % ROW_BLOCK == 0
    a = w_out.shape[1] // 2
    n_experts = w_router.shape[2]

    meta_blk = jnp.concatenate([jnp.zeros((ROW_BLOCK - N_META, d), x.dtype), meta_tokens], axis=0)
    w = w_in[0]
    scale = HEAD_DIM ** -0.5 * LOG2_E
    wqt = (w[:, :a] * scale).T.astype(BF16)
    wk = w[:, a:2 * a].astype(BF16)
    wvt = w[:, 2 * a:3 * a].T.astype(BF16)
    wc = w[:, 3 * a:].astype(BF16)
    qt, k5, v5, convn = _in_proj(x, meta_blk, g_mix, wqt, wk, wvt, wc, conv_w[0], conv_b, g_conv_out)
    att_t = _attention(qt, k5, v5)

    wo = w_out[0].astype(BF16)
    wr = w_router[0]
    wr_hi = wr.astype(BF16)
    wr_split = jnp.stack([wr_hi, (wr - wr_hi.astype(F32)).astype(BF16)])
    h1, xn, top_e, gates = _post(att_t, convn, x, meta_tokens, g_attn_out, wo[:a], wo[a:], g_ffn, wr_split, b_router)

    base, cnt, npass, rend, block_expert, n_used, cap = _route_tables(top_e, n_experts)
    top_et, gates_t = top_e.T, gates.T
    xs = _dispatch(top_et, xn, base, cnt, npass, rend, cap)
    ys = _ffn(xs, block_expert, n_used, w_gu[0], b_gu[0], w_down[0], b_down[0])
    out = _combine(h1, top_et, gates_t, ys, base, cnt, npass, g_final.reshape(1, d), b * seq, n_experts)
    return out.reshape(b, seq, d)
```

```python
import functools

import jax
import jax.numpy as jnp
from jax import lax
from jax.experimental import pallas as pl
from jax.experimental.pallas import tpu as pltpu

N_META = 16
HEAD_DIM = 64
CONV_K = 3
TOP_K = 4
SWIGLU_LIMIT = 7.0
SWIGLU_ALPHA = 1.702
RMS_EPS = 1e-6

SUBLANES = 8
ROW_BLOCK = 256
KEY_CHUNK = 128
KEY_GROUPS = KEY_CHUNK // SUBLANES
ATTN_CHUNKS_PER_ITER = ROW_BLOCK // KEY_CHUNK
ATTN_HEADS_PER_STEP = 2
STICK_DEAD_BITS = 160.0
LOG2_E = 1.4426950408889634
MOE_ROWS = 256
TOK_BLOCK = 128
SLAB = 32
SLAB_SHORT = 24
CARRY_ROWS = SUBLANES
VMEM_LIMIT = 48 * 1024 * 1024

F32 = jnp.float32
BF16 = jnp.bfloat16


def _rms(x, g):
    return x * lax.rsqrt(jnp.mean(x * x, axis=-1, keepdims=True) + RMS_EPS) * g


def _in_proj_kernel(x_ref, meta_ref, g_ref, perm_ref, wqv_ref, wkc_ref, cw_ref, cb_ref, gc_ref,
                    qt_ref, k_ref, vt_ref, conv_ref, u_scr):
    j = pl.program_id(1)
    rows = x_ref.shape[1]
    cw = conv_ref.shape[2]
    h = jnp.where(j == 0, meta_ref[...], x_ref[0])
    nb = _rms(h, g_ref[...]).astype(BF16)
    nt_dims = (((1,), (1,)), ((), ()))
    a = qt_ref.shape[2]
    qv = lax.dot_general(wqv_ref[...], nb, nt_dims, preferred_element_type=F32)
    kc = jnp.dot(nb, wkc_ref[...], preferred_element_type=F32)
    qt_ref[0, 0] = qv[:a].astype(BF16)
    kp = jnp.dot(perm_ref[...], kc[:, :a].astype(BF16), preferred_element_type=F32).astype(BF16)
    vtp = lax.dot_general(qv[a:].astype(BF16), perm_ref[...], nt_dims, preferred_element_type=F32).astype(BF16)
    for hh in range(k_ref.shape[1]):
        for c in range(rows // KEY_CHUNK):
            k_ref[0, hh, c] = kp[c * KEY_CHUNK:(c + 1) * KEY_CHUNK, hh * HEAD_DIM:(hh + 1) * HEAD_DIM]
            vt_ref[0, hh, c] = vtp[hh * HEAD_DIM:(hh + 1) * HEAD_DIM, c * KEY_CHUNK:(c + 1) * KEY_CHUNK]
    cp = kc[:, a:]
    gate_b, u = cp[:, :cw], cp[:, cw:2 * cw] * cp[:, 2 * cw:]

    @pl.when(j == 0)
    def _():
        u_scr[0:CARRY_ROWS, :] = jnp.zeros((CARRY_ROWS, cw), F32)

    u_scr[CARRY_ROWS:CARRY_ROWS + rows, :] = u
    u1 = u_scr[CARRY_ROWS - 1:CARRY_ROWS - 1 + rows, :]
    u2 = u_scr[CARRY_ROWS - 2:CARRY_ROWS - 2 + rows, :]
    y = cw_ref[0:1, :] * u2 + cw_ref[1:2, :] * u1 + cw_ref[2:3, :] * u
    u_scr[0:CARRY_ROWS, :] = u_scr[rows:rows + CARRY_ROWS, :]
    conv = gate_b * (y + cb_ref[...])
    conv_ref[0] = _rms(conv, gc_ref[...]).astype(BF16)


def _key_order_permutation():
    p = jnp.arange(ROW_BLOCK)
    chunk, g, r = p // KEY_CHUNK, (p % KEY_CHUNK) // SUBLANES, p % SUBLANES
    src = chunk * KEY_CHUNK + r * KEY_GROUPS + g
    return (src[:, None] == jnp.arange(ROW_BLOCK)[None, :]).astype(BF16)


def _in_proj(x, meta_blk, g_mix, wqv, wkc, conv_w, conv_b, g_conv):
    b, seq, d = x.shape
    a = wqv.shape[0] // 2
    cw = (wkc.shape[1] - a) // 3
    heads = a // HEAD_DIM
    nblk = seq // ROW_BLOCK + 1
    tp = nblk * ROW_BLOCK
    cpb = ROW_BLOCK // KEY_CHUNK
    full = lambda shp: pl.BlockSpec(shp, lambda bi, j: (0,) * len(shp))
    return pl.pallas_call(
        _in_proj_kernel,
        grid=(b, nblk),
        in_specs=[
            pl.BlockSpec((1, ROW_BLOCK, d), lambda bi, j: (bi, jnp.maximum(j - 1, 0), 0)),
            full((ROW_BLOCK, d)), full((1, d)), full((ROW_BLOCK, ROW_BLOCK)),
            full((2 * a, d)), full((d, a + 3 * cw)),
            full((CONV_K, cw)), full((1, cw)), full((1, cw)),
        ],
        out_specs=[
            pl.BlockSpec((1, 1, a, ROW_BLOCK), lambda bi, j: (bi, j, 0, 0)),
            pl.BlockSpec((1, heads, cpb, KEY_CHUNK, HEAD_DIM), lambda bi, j: (bi, 0, j, 0, 0)),
            pl.BlockSpec((1, heads, cpb, HEAD_DIM, KEY_CHUNK), lambda bi, j: (bi, 0, j, 0, 0)),
            pl.BlockSpec((1, ROW_BLOCK, cw), lambda bi, j: (bi, j, 0)),
        ],
        out_shape=[
            jax.ShapeDtypeStruct((b, nblk, a, ROW_BLOCK), BF16),
            jax.ShapeDtypeStruct((b, heads, tp // KEY_CHUNK, KEY_CHUNK, HEAD_DIM), BF16),
            jax.ShapeDtypeStruct((b, heads, tp // KEY_CHUNK, HEAD_DIM, KEY_CHUNK), BF16),
            jax.ShapeDtypeStruct((b, tp, cw), BF16),
        ],
        scratch_shapes=[pltpu.VMEM((CARRY_ROWS + ROW_BLOCK, cw), F32)],
        compiler_params=pltpu.CompilerParams(
            dimension_semantics=("parallel", "arbitrary"), vmem_limit_bytes=VMEM_LIMIT),
        name="in_proj",
    )(x, meta_blk, g_mix, _key_order_permutation(), wqv, wkc, conv_w, conv_b, g_conv)


def _attn_kernel(qt_ref, k_ref, vt_ref, o_ref, z_scr, zd_scr, w_scr):
    nblk, nq = qt_ref.shape[1], qt_ref.shape[3]
    hd = HEAD_DIM
    heads = range(ATTN_HEADS_PER_STEP)
    per_iter = ATTN_CHUNKS_PER_ITER
    shape3 = (KEY_GROUPS, SUBLANES, nq)
    key_in_chunk = (lax.broadcasted_iota(jnp.int32, shape3, 1) * KEY_GROUPS
                    + lax.broadcasted_iota(jnp.int32, shape3, 0))
    lane = lax.broadcasted_iota(jnp.int32, shape3, 2)
    sub = lax.broadcasted_iota(jnp.int32, (SUBLANES, nq), 0)
    step_top = lambda i, t: (i + 1 - t) * per_iter - 1

    def scores(dst, i, jtop):
        for h in heads:
            qt = qt_ref[0, i, h * hd:(h + 1) * hd, :]
            for c in range(per_iter):
                dst[h, c] = jnp.dot(k_ref[0, h, jtop - c], qt, preferred_element_type=F32)

    def weights(src, carries, i, jtop, masked):
        return tuple(head_weights(src, h, carries[h], i, jtop, masked) for h in heads)

    def head_weights(src, h, carry, i, jtop, masked):
        for c in range(per_iter):
            z = src[h, c].reshape(shape3)
            sp = jnp.maximum(z, jnp.log(1.0 + jnp.exp2(jnp.minimum(z, 64.0))) * LOG2_E)
            if masked:
                valid = (jtop - c) * KEY_CHUNK + key_in_chunk < i * nq + lane
                sp = jnp.where(valid, sp, 0.0)
            runs = [None] * KEY_GROUPS
            run = sp[KEY_GROUPS - 1]
            runs[KEY_GROUPS - 1] = run
            for g in range(KEY_GROUPS - 2, -1, -1):
                run = run + sp[g]
                runs[g] = run
            tail = run
            for shift in (1, 2, 4):
                tail = tail + jnp.where(sub < SUBLANES - shift, pltpu.roll(tail, SUBLANES - shift, 0), 0.0)
            off = carry + (tail - run)
            w = jnp.exp2(z - (jnp.stack(runs, axis=0) + off[None]))
            if masked:
                w = jnp.where(valid, w, 0.0)
            w_scr[h, c] = w.reshape(KEY_CHUNK, nq).astype(BF16)
            carry = carry + tail[0:1, :]
        return carry

    def weighted_values(accs, jtop):
        accs = list(accs)
        for h in heads:
            for c in range(per_iter):
                accs[h] = accs[h] + jnp.dot(vt_ref[0, h, jtop - c], w_scr[h, c], preferred_element_type=F32)
        return tuple(accs)

    def store(i, accs):
        for h in heads:
            o_ref[0, i, h * hd:(h + 1) * hd, :] = accs[h]

    zero_carry = tuple(jnp.zeros((1, nq), F32) for _ in heads)
    zero_acc = tuple(jnp.zeros((hd, nq), F32) for _ in heads)
    w_scr[...] = jnp.zeros(w_scr.shape, BF16)
    scores(zd_scr, 0, step_top(0, 0))

    def stick_left(carries):
        m = carries[0]
        for h in heads[1:]:
            m = jnp.minimum(m, carries[h])
        return jnp.min(m) < STICK_DEAD_BITS

    def query_block(i, state):
        accs, pending_top = state
        store(jnp.maximum(i - 1, 0), weighted_values(accs, pending_top))
        carries = weights(zd_scr, zero_carry, i, step_top(i, 0), True)
        nxt = jnp.minimum(i + 1, nblk - 1)
        scores(zd_scr, nxt, step_top(nxt, 0))
        scores(z_scr, i, jnp.maximum(step_top(i, 1), per_iter - 1))

        def step(state):
            t, _, carries, accs = state
            live = stick_left(carries)
            accs = weighted_values(accs, step_top(i, t - 1))
            carries = weights(z_scr, carries, i, step_top(i, t), False)
            scores(z_scr, i, jnp.maximum(step_top(i, t + 1), per_iter - 1))
            return t + 1, live, carries, accs

        t_end, _, _, accs = lax.while_loop(lambda s: (s[0] <= i) & s[1], step,
                                           (jnp.int32(1), jnp.bool_(True), carries, zero_acc))
        return accs, step_top(i, t_end - 1)

    accs, pending_top = lax.fori_loop(0, nblk, query_block, (zero_acc, jnp.int32(per_iter - 1)))
    store(nblk - 1, weighted_values(accs, pending_top))


def _attention(qt, k5, v5):
    b, nblk, a, nq = qt.shape
    heads = a // HEAD_DIM
    nchunks = k5.shape[2]
    hps = ATTN_HEADS_PER_STEP
    return pl.pallas_call(
        _attn_kernel,
        grid=(b, heads // hps),
        in_specs=[
            pl.BlockSpec((1, nblk, hps * HEAD_DIM, nq), lambda bi, h: (bi, 0, h, 0)),
            pl.BlockSpec((1, hps, nchunks, KEY_CHUNK, HEAD_DIM), lambda bi, h: (bi, h, 0, 0, 0)),
            pl.BlockSpec((1, hps, nchunks, HEAD_DIM, KEY_CHUNK), lambda bi, h: (bi, h, 0, 0, 0)),
        ],
        out_specs=pl.BlockSpec((1, nblk, hps * HEAD_DIM, nq), lambda bi, h: (bi, 0, h, 0)),
        out_shape=jax.ShapeDtypeStruct((b, nblk, a, nq), F32),
        scratch_shapes=[pltpu.VMEM((hps, ATTN_CHUNKS_PER_ITER, KEY_CHUNK, nq), F32),
                        pltpu.VMEM((hps, ATTN_CHUNKS_PER_ITER, KEY_CHUNK, nq), F32),
                        pltpu.VMEM((hps, ATTN_CHUNKS_PER_ITER, KEY_CHUNK, nq), BF16)],
        compiler_params=pltpu.CompilerParams(
            dimension_semantics=("parallel", "parallel"), vmem_limit_bytes=VMEM_LIMIT),
        name="stickbreak_attn",
    )(qt, k5, v5)


def _top_k_gates(logits):
    n, e = logits.shape
    lane = lax.broadcasted_iota(jnp.int32, (n, e), 1)
    slot = lax.broadcasted_iota(jnp.int32, (n, TOP_K), 1)
    top_e = jnp.zeros((n, TOP_K), jnp.int32)
    top_l = jnp.zeros((n, TOP_K), F32)
    cur = logits
    for kk in range(TOP_K):
        m = jnp.max(cur, axis=-1, keepdims=True)
        idx = jnp.min(jnp.where(cur == m, lane, e), axis=-1, keepdims=True)
        top_e = jnp.where(slot == kk, idx, top_e)
        top_l = jnp.where(slot == kk, m, top_l)
        cur = jnp.where(lane == idx, -jnp.inf, cur)
    ex = jnp.exp(top_l - top_l[:, 0:1])
    return top_e, ex / jnp.sum(ex, axis=-1, keepdims=True)


def _post_kernel(at_ref, conv_ref, h_ref, ga_ref, wa_ref, wcv_ref, gf_ref, wr_ref, br_ref, *rest, meta):
    h1_ref, xn_ref, te_ref, gt_ref = rest[-4:]
    att = at_ref[0, 0].T
    convn = conv_ref[0]
    h = h_ref[...] if meta else h_ref[0]
    if meta:
        att, convn = att[ROW_BLOCK - N_META:, :], convn[ROW_BLOCK - N_META:, :]
    attn = _rms(att, ga_ref[...]).astype(BF16)
    h1 = (h + jnp.dot(attn, wa_ref[...], preferred_element_type=F32)
          + jnp.dot(convn, wcv_ref[...], preferred_element_type=F32))
    xn = _rms(h1, gf_ref[...])
    xh = xn.astype(BF16)
    xl = (xn - xh.astype(F32)).astype(BF16)
    logits = (jnp.dot(xh, wr_ref[0], preferred_element_type=F32) + jnp.dot(xl, wr_ref[0], preferred_element_type=F32)
              + jnp.dot(xh, wr_ref[1], preferred_element_type=F32) + br_ref[...])
    top_e, gates = _top_k_gates(logits)
    h1_ref[...] = h1
    xn_ref[...] = xh
    te_ref[...] = top_e
    gt_ref[...] = gates


def _post(att_t, convn, x, meta_tokens, g_attn, w_out_a, w_out_c, g_ffn, w_router, b_router):
    b, seq, d = x.shape
    a = att_t.shape[2]
    cw = convn.shape[2]
    nreal = b * seq
    ntok = nreal + b * N_META
    nblk = seq // ROW_BLOCK
    out_shape = [
        jax.ShapeDtypeStruct((ntok, d), F32),
        jax.ShapeDtypeStruct((ntok, d), BF16),
        jax.ShapeDtypeStruct((ntok, TOP_K), jnp.int32),
        jax.ShapeDtypeStruct((ntok, TOP_K), F32),
    ]
    weights = (g_attn, w_out_a, w_out_c, g_ffn, w_router, b_router)

    def wspecs(nargs):
        return [pl.BlockSpec(w.shape, lambda *idx, n=w.ndim: (0,) * n) for w in weights]

    def ospecs(rows, index):
        return [pl.BlockSpec((rows, d), index), pl.BlockSpec((rows, d), index),
                pl.BlockSpec((rows, TOP_K), index), pl.BlockSpec((rows, TOP_K), index)]

    real = pl.pallas_call(
        functools.partial(_post_kernel, meta=False),
        grid=(b, nblk),
        in_specs=[
            pl.BlockSpec((1, 1, a, ROW_BLOCK), lambda bi, j: (bi, j + 1, 0, 0)),
            pl.BlockSpec((1, ROW_BLOCK, cw), lambda bi, j: (bi, j + 1, 0)),
            pl.BlockSpec((1, ROW_BLOCK, d), lambda bi, j: (bi, j, 0)),
        ] + wspecs(2),
        out_specs=ospecs(ROW_BLOCK, lambda bi, j: (bi * nblk + j, 0)),
        out_shape=out_shape,
        compiler_params=pltpu.CompilerParams(
            dimension_semantics=("parallel", "parallel"), vmem_limit_bytes=VMEM_LIMIT),
        name="post_attn_real",
    )(att_t, convn, x, *weights)

    meta_blk0 = nreal // N_META
    return pl.pallas_call(
        functools.partial(_post_kernel, meta=True),
        grid=(b,),
        in_specs=[
            pl.BlockSpec((1, 1, a, ROW_BLOCK), lambda bi: (bi, 0, 0, 0)),
            pl.BlockSpec((1, ROW_BLOCK, cw), lambda bi: (bi, 0, 0)),
            pl.BlockSpec((N_META, d), lambda bi: (0, 0)),
        ] + wspecs(1) + [pl.BlockSpec(memory_space=pl.ANY)] * 4,
        out_specs=ospecs(N_META, lambda bi: (meta_blk0 + bi, 0)),
        out_shape=out_shape,
        input_output_aliases={9: 0, 10: 1, 11: 2, 12: 3},
        compiler_params=pltpu.CompilerParams(
            dimension_semantics=("parallel",), vmem_limit_bytes=VMEM_LIMIT),
        name="post_attn_meta",
    )(att_t, convn, meta_tokens, *weights, *real)


def _rank_one_hot(te, n_experts, shift):
    t = te.shape[1]
    eio = lax.broadcasted_iota(jnp.int32, (n_experts, t), 0)
    ind = te[0:1, :] == eio
    for kk in range(1, TOP_K):
        ind = ind | (te[kk:kk + 1, :] == eio)
    before = (lax.broadcasted_iota(jnp.int32, (t, t), 0) < lax.broadcasted_iota(jnp.int32, (t, t), 1)).astype(BF16)
    rank = jnp.dot(ind.astype(BF16), before, preferred_element_type=F32)
    rank = jnp.where(ind, rank - shift, -1.0)
    rio = lax.broadcasted_iota(jnp.int32, (SLAB, t), 0).astype(F32)
    sel = [jnp.broadcast_to(rank[e:e + 1, :], (SLAB, t)) == rio for e in range(n_experts)]
    return jnp.concatenate(sel, axis=0), ind


def _dispatch_kernel(base_ref, cnt_ref, npass_ref, rend_ref, te_ref, xn_ref, xs_hbm, slab, zbuf, sem, pend):
    b = pl.program_id(0)
    n_experts = rend_ref.shape[0]
    lt = xn_ref.shape[1] // 128
    srows = SLAB * lt
    slot = b % 2

    def slab_copy(sl, e, p, full):
        nrows = srows if full else SLAB_SHORT * lt
        dst = pl.multiple_of((base_ref[b * n_experts + e] + p * SLAB) * lt, lt)
        return pltpu.make_async_copy(slab.at[sl, pl.ds(e * srows, nrows), :], xs_hbm.at[pl.ds(dst, nrows), :],
                                     sem.at[sl, int(full)])

    def wait_slot(sl):
        for full in (False, True):
            @pl.loop(0, pend[sl, int(full)])
            def _(_):
                slab_copy(sl, 0, 0, full).wait()
            pend[sl, int(full)] = 0

    @pl.when(b == 0)
    def _():
        zbuf[...] = jnp.zeros(zbuf.shape, F32)
        zrows = zbuf.shape[0]

        def clear(e):
            dst = pl.multiple_of(jnp.maximum(rend_ref[e] * lt - zrows, 0), lt)
            return pltpu.make_async_copy(zbuf, xs_hbm.at[pl.ds(dst, zrows), :], sem.at[0, 0])

        for e in range(n_experts):
            clear(e).start()
            clear(e).wait()
        for sl in range(2):
            pend[sl, 0] = 0
            pend[sl, 1] = 0

    def fill(sl, p):
        sel, _ = _rank_one_hot(te_ref[...], n_experts, (p * SLAB).astype(F32))
        rows = jnp.dot(sel.astype(BF16), xn_ref[...], preferred_element_type=F32)
        for c in range(lt):
            slab[sl, pl.ds(c, n_experts * SLAB, stride=lt), :] = rows[:, c * 128:(c + 1) * 128]

    fill(slot, jnp.int32(0))
    wait_slot(1 - slot)
    n_full = jnp.int32(0)
    for e in range(n_experts):
        full = cnt_ref[b * n_experts + e] > SLAB_SHORT
        n_full = n_full + full.astype(jnp.int32)

        @pl.when(full)
        def _():
            slab_copy(slot, e, 0, True).start()

        @pl.when(jnp.logical_not(full))
        def _():
            slab_copy(slot, e, 0, False).start()
    pend[slot, 1] = n_full
    pend[slot, 0] = n_experts - n_full

    @pl.loop(1, npass_ref[b])
    def _(p):
        wait_slot(slot)
        fill(slot, p)
        for e in range(n_experts):
            @pl.when(cnt_ref[b * n_experts + e] > p * SLAB)
            def _():
                cp = slab_copy(slot, e, p, True)
                cp.start()
                cp.wait()

    @pl.when(b == pl.num_programs(0) - 1)
    def _():
        wait_slot(slot)
        wait_slot(1 - slot)


def _dispatch(top_et, xn, base, cnt, npass, rend, cap):
    ntok, d = xn.shape
    n_experts = rend.shape[0]
    lt = d // 128
    assert d % 128 == 0 and lt % SUBLANES == 0 and ntok % TOK_BLOCK == 0
    grid_spec = pltpu.PrefetchScalarGridSpec(
        num_scalar_prefetch=4,
        grid=(ntok // TOK_BLOCK,),
        in_specs=[pl.BlockSpec((TOP_K, TOK_BLOCK), lambda b, *_: (0, b)),
                  pl.BlockSpec((TOK_BLOCK, d), lambda b, *_: (b, 0))],
        out_specs=pl.BlockSpec(memory_space=pl.ANY),
        scratch_shapes=[pltpu.VMEM((2, n_experts * SLAB * lt, 128), F32),
                        pltpu.VMEM((2 * MOE_ROWS * lt, 128), F32),
                        pltpu.SemaphoreType.DMA((2, 2)),
                        pltpu.SMEM((2, 2), jnp.int32)],
    )
    return pl.pallas_call(
        _dispatch_kernel,
        grid_spec=grid_spec,
        out_shape=jax.ShapeDtypeStruct((cap * lt, 128), F32),
        compiler_params=pltpu.CompilerParams(dimension_semantics=("arbitrary",), vmem_limit_bytes=VMEM_LIMIT),
        name="moe_dispatch",
    )(base, cnt, npass, rend, top_et, xn)


def _ffn_kernel(bexp_ref, nused_ref, xs_ref, wgu_ref, bgu_ref, wd_ref, bd_ref, ys_ref, wgu_bf, wd_bf):
    i = pl.program_id(0)
    ff, d = wd_bf.shape
    lt = d // 128

    @pl.when(i < nused_ref[0])
    def _():
        @pl.when((i == 0) | (bexp_ref[i] != bexp_ref[jnp.maximum(i - 1, 0)]))
        def _():
            wgu_bf[...] = wgu_ref[0].astype(BF16)
            wd_bf[...] = wd_ref[0].astype(BF16)

        x = jnp.concatenate([xs_ref[pl.ds(c, MOE_ROWS, stride=lt), :] for c in range(lt)], axis=1).astype(BF16)
        gu = jnp.dot(x, wgu_bf[...], preferred_element_type=F32) + bgu_ref[0]
        gate = jnp.minimum(gu[:, :ff], SWIGLU_LIMIT)
        lin = jnp.clip(gu[:, ff:], -SWIGLU_LIMIT, SWIGLU_LIMIT)
        hid = (lin + 1.0) * (gate * jax.nn.sigmoid(SWIGLU_ALPHA * gate))
        y = jnp.dot(hid.astype(BF16), wd_bf[...], preferred_element_type=F32) + bd_ref[0]
        for c in range(lt):
            ys_ref[pl.ds(c, MOE_ROWS, stride=lt), :] = y[:, c * 128:(c + 1) * 128]

    @pl.when(i >= nused_ref[0])
    def _():
        ys_ref[...] = jnp.zeros(ys_ref.shape, F32)


def _ffn(xs, block_expert, n_used, w_gu, b_gu, w_down, b_down):
    e, d, ff2 = w_gu.shape
    ff = ff2 // 2
    lt = d // 128
    nblocks = block_expert.shape[0]
    rows = MOE_ROWS * lt
    grid_spec = pltpu.PrefetchScalarGridSpec(
        num_scalar_prefetch=2,
        grid=(nblocks,),
        in_specs=[
            pl.BlockSpec((rows, 128), lambda i, be, nu: (i, 0)),
            pl.BlockSpec((1, d, ff2), lambda i, be, nu: (be[i], 0, 0)),
            pl.BlockSpec((1, 1, ff2), lambda i, be, nu: (be[i], 0, 0)),
            pl.BlockSpec((1, ff, d), lambda i, be, nu: (be[i], 0, 0)),
            pl.BlockSpec((1, 1, d), lambda i, be, nu: (be[i], 0, 0)),
        ],
        out_specs=pl.BlockSpec((rows, 128), lambda i, be, nu: (i, 0)),
        scratch_shapes=[pltpu.VMEM((d, ff2), BF16), pltpu.VMEM((ff, d), BF16)],
    )
    return pl.pallas_call(
        _ffn_kernel,
        grid_spec=grid_spec,
        out_shape=jax.ShapeDtypeStruct(xs.shape, F32),
        compiler_params=pltpu.CompilerParams(dimension_semantics=("arbitrary",), vmem_limit_bytes=VMEM_LIMIT),
        name="moe_experts",
    )(block_expert, n_used, xs, w_gu, b_gu.reshape(e, 1, ff2), w_down, b_down.reshape(e, 1, d))


def _route_tables(top_e, n_experts):
    ntok = top_e.shape[0]
    nb = ntok // TOK_BLOCK
    ids = jnp.arange(n_experts, dtype=jnp.int32)
    cnt = jnp.sum(top_e.reshape(nb, TOK_BLOCK * TOP_K, 1) == ids, axis=1, dtype=jnp.int32)
    total = jnp.sum(cnt, axis=0)
    region = (total + SLAB + MOE_ROWS - 1) // MOE_ROWS * MOE_ROWS
    rend = jnp.cumsum(region)
    base = (rend - region)[None, :] + jnp.cumsum(cnt, axis=0) - cnt
    npass = jnp.maximum((jnp.max(cnt, axis=1) + SLAB - 1) // SLAB, 1)
    nblocks = -(-(ntok * TOP_K + n_experts * (SLAB + MOE_ROWS - 1)) // MOE_ROWS)
    block_start = jnp.arange(nblocks, dtype=jnp.int32) * MOE_ROWS
    block_expert = jnp.minimum(jnp.sum(rend[None, :] <= block_start[:, None], axis=1), n_experts - 1)
    n_used = (rend[-1] // MOE_ROWS).reshape(1)
    i32 = lambda v: v.astype(jnp.int32)
    return (i32(base.reshape(-1)), i32(cnt.reshape(-1)), i32(npass), i32(rend), i32(block_expert), i32(n_used),
            nblocks * MOE_ROWS)


def _combine_kernel(base_ref, cnt_ref, npass_ref, te_ref, gt_ref, h1_ref, g_ref, ys_hbm, o_ref, ybuf, acc, sem):
    b = pl.program_id(0)
    nb = pl.num_programs(0)
    n_experts = ybuf.shape[1] // (SLAB * (h1_ref.shape[1] // 128))
    lt = h1_ref.shape[1] // 128
    srows = SLAB * lt
    slot = b % 2

    def slab_copy(blk, sl, e, p):
        src = pl.multiple_of((base_ref[blk * n_experts + e] + p * SLAB) * lt, lt)
        return pltpu.make_async_copy(ys_hbm.at[pl.ds(src, srows), :], ybuf.at[sl, pl.ds(e * srows, srows), :],
                                     sem.at[sl])

    def fetch(blk, sl):
        for e in range(n_experts):
            slab_copy(blk, sl, e, 0).start()

    @pl.when(b == 0)
    def _():
        fetch(b, slot)

    @pl.when(b + 1 < nb)
    def _():
        fetch(b + 1, 1 - slot)

    for e in range(n_experts):
        slab_copy(b, slot, e, 0).wait()

    def gathered(p):
        te = te_ref[...]
        sel, ind = _rank_one_hot(te, n_experts, (p * SLAB).astype(F32))
        eio = lax.broadcasted_iota(jnp.int32, ind.shape, 0)
        gate_te = jnp.zeros(ind.shape, F32)
        for kk in range(TOP_K):
            gate_te = gate_te + jnp.where(te[kk:kk + 1, :] == eio, gt_ref[kk:kk + 1, :], 0.0)
        gate_rows = jnp.concatenate(
            [jnp.sum(jnp.where(sel[e * SLAB:(e + 1) * SLAB], gate_te[e:e + 1, :], 0.0), axis=1, keepdims=True)
             for e in range(n_experts)], axis=0)
        y = jnp.concatenate([ybuf[slot, pl.ds(c, n_experts * SLAB, stride=lt), :] for c in range(lt)], axis=1)
        y = y * gate_rows
        hi = y.astype(BF16)
        lo = (y - hi.astype(F32)).astype(BF16)
        tn_dims = (((0,), (0,)), ((), ()))
        onehot = sel.astype(BF16)
        return (lax.dot_general(onehot, hi, tn_dims, preferred_element_type=F32)
                + lax.dot_general(onehot, lo, tn_dims, preferred_element_type=F32))

    acc[...] = h1_ref[...] + gathered(jnp.int32(0))

    @pl.loop(1, npass_ref[b])
    def _(p):
        for e in range(n_experts):
            @pl.when(cnt_ref[b * n_experts + e] > p * SLAB)
            def _():
                cp = slab_copy(b, slot, e, p)
                cp.start()
                cp.wait()
        acc[...] += gathered(p)

    o_ref[...] = _rms(acc[...], g_ref[...])


def _combine(h1, top_et, gates_t, ys, base, cnt, npass, g_final, nreal, n_experts):
    d = h1.shape[1]
    lt = d // 128
    grid_spec = pltpu.PrefetchScalarGridSpec(
        num_scalar_prefetch=3,
        grid=(nreal // TOK_BLOCK,),
        in_specs=[pl.BlockSpec((TOP_K, TOK_BLOCK), lambda b, *_: (0, b)),
                  pl.BlockSpec((TOP_K, TOK_BLOCK), lambda b, *_: (0, b)),
                  pl.BlockSpec((TOK_BLOCK, d), lambda b, *_: (b, 0)),
                  pl.BlockSpec((1, d), lambda b, *_: (0, 0)),
                  pl.BlockSpec(memory_space=pl.ANY)],
        out_specs=pl.BlockSpec((TOK_BLOCK, d), lambda b, *_: (b, 0)),
        scratch_shapes=[pltpu.VMEM((2, n_experts * SLAB * lt, 128), F32),
                        pltpu.VMEM((TOK_BLOCK, d), F32),
                        pltpu.SemaphoreType.DMA((2,))],
    )
    return pl.pallas_call(
        _combine_kernel,
        grid_spec=grid_spec,
        out_shape=jax.ShapeDtypeStruct((nreal, d), F32),
        compiler_params=pltpu.CompilerParams(dimension_semantics=("arbitrary",), vmem_limit_bytes=VMEM_LIMIT),
        name="moe_combine",
    )(base, cnt, npass, top_et, gates_t, h1, g_final, ys)


def kernel(x, meta_tokens, g_mix, w_in, g_attn_out, conv_w, conv_b, g_conv_out, w_out, g_ffn,
           w_router, b_router, w_gu, b_gu, w_down, b_down, g_final):
    b, seq, d = x.shape
    depth = w_in.shape[0]
    assert depth == 1 and seq % ROW_BLOCK == 0
    a = w_out.shape[1] // 2
    n_experts = w_router.shape[2]

    meta_blk = jnp.concatenate([jnp.zeros((ROW_BLOCK - N_META, d), x.dtype), meta_tokens], axis=0)
    w = w_in[0]
    scale = HEAD_DIM ** -0.5 * LOG2_E
    wqv = jnp.concatenate([w[:, :a] * scale, w[:, 2 * a:3 * a]], axis=1).T.astype(BF16)
    wkc = jnp.concatenate([w[:, a:2 * a], w[:, 3 * a:]], axis=1).astype(BF16)
    qt, k5, v5, convn = _in_proj(x, meta_blk, g_mix, wqv, wkc, conv_w[0], conv_b, g_conv_out)
    att_t = _attention(qt, k5, v5)

    wo = w_out[0].astype(BF16)
    wr = w_router[0]
    wr_hi = wr.astype(BF16)
    wr_split = jnp.stack([wr_hi, (wr - wr_hi.astype(F32)).astype(BF16)])
    h1, xn, top_e, gates = _post(att_t, convn, x, meta_tokens, g_attn_out, wo[:a], wo[a:], g_ffn, wr_split, b_router)

    base, cnt, npass, rend, block_expert, n_used, cap = _route_tables(top_e, n_experts)
    top_et, gates_t = top_e.T, gates.T
    xs = _dispatch(top_et, xn, base, cnt, npass, rend, cap)
    ys = _ffn(xs, block_expert, n_used, w_gu[0], b_gu[0], w_down[0], b_down[0])
    out = _combine(h1, top_et, gates_t, ys, base, cnt, npass, g_final.reshape(1, d), b * seq, n_experts)
    return out.reshape(b, seq, d)
```
